```python
import jax, jax.numpy as jnp
from jax import lax
import numpy as np

D_MODEL = 1024
BATCH = 4
SEQ = 8192
DEPTH = 1

CHUNK = 64
MIX_WIDTH = D_MODEL
CONV_WIDTH = D_MODEL // 2
CONV_HEADS = 8
CONV_K = 31
GMLP_WIDTH = D_MODEL // 2
GMLP_HEADS = 8
GMLP_HEAD_DIM = GMLP_WIDTH // GMLP_HEADS
GMLP_CHUNK = 2 * CHUNK
IN_WIDTH = 2 * CONV_WIDTH + 2 * GMLP_WIDTH
D_FF = 2816
FFN_CONV_K = 3
NORM_EPS = 1e-6
LN_EPS = 1e-5

kernel_name = "hybrid_conformer_gmlp_convffn_block"


def rms_norm(x, g):
    xf = x.astype(jnp.float32)
    r = lax.rsqrt(jnp.mean(xf * xf, axis=-1, keepdims=True) + NORM_EPS)
    return (xf * r).astype(x.dtype) * g


def layer_norm(x, g, b):
    xf = x.astype(jnp.float32)
    mu = jnp.mean(xf, axis=-1, keepdims=True)
    var = jnp.mean(jnp.square(xf - mu), axis=-1, keepdims=True)
    return ((xf - mu) * lax.rsqrt(var + LN_EPS)).astype(x.dtype) * g + b


def causal_dwconv(x, w, b):
    k, c = w.shape
    y = lax.conv_general_dilated(
        x, w[:, None, :].astype(x.dtype), window_strides=(1,),
        padding=[(k - 1, 0)], dimension_numbers=("NWC", "WIO", "NWC"),
        feature_group_count=c)
    return y + b


def conformer_conv_group(a_val, a_gate, conv_w, conv_b, ln_g, ln_b):
    a = a_val * jax.nn.sigmoid(a_gate)
    a = causal_dwconv(a, conv_w, conv_b)
    a = layer_norm(a, ln_g, ln_b)
    return jax.nn.silu(a)


def gmlp_spatial_group(b_u, b_v, ln_g, ln_b, w_s, b_s):
    bsz, seq, _ = b_u.shape
    nb = seq // GMLP_CHUNK
    u = jax.nn.gelu(b_u, approximate=False)
    v = layer_norm(jax.nn.gelu(b_v, approximate=False), ln_g, ln_b)
    v = v.reshape(bsz, nb, GMLP_CHUNK, GMLP_HEADS, GMLP_HEAD_DIM)
    u = u.reshape(bsz, nb, GMLP_CHUNK, GMLP_HEADS, GMLP_HEAD_DIM)
    mask = jnp.tril(jnp.ones((GMLP_CHUNK, GMLP_CHUNK), dtype=w_s.dtype))
    w_c = w_s * mask[None]
    sp = jnp.einsum("hts,bnshd->bnthd", w_c, v)
    sp = sp + jnp.transpose(b_s)[None, None, :, :, None]
    return (u * sp).reshape(bsz, seq, GMLP_WIDTH)


def setup_inputs(seed: int = 0) -> dict:
    key = jax.random.key(seed)
    ks = jax.random.split(key, 24)
    f32 = jnp.float32
    nrm = lambda k, s, scale: jax.random.normal(k, s, f32) * scale
    L = DEPTH
    return {
        "x": jax.random.normal(ks[0], (BATCH, SEQ, D_MODEL), f32),
        "mix_norm_g": 1.0 + nrm(ks[1], (L, D_MODEL), 0.02),
        "w_in": nrm(ks[2], (L, D_MODEL, IN_WIDTH), D_MODEL ** -0.5),
        "b_in": nrm(ks[3], (L, IN_WIDTH), 0.02),
        "conv_a_w": nrm(ks[4], (L, CONV_K, CONV_WIDTH), CONV_K ** -0.5),
        "conv_a_b": nrm(ks[5], (L, CONV_WIDTH), 0.02),
        "ln_a_g": 1.0 + nrm(ks[6], (L, CONV_WIDTH), 0.02),
        "ln_a_b": nrm(ks[7], (L, CONV_WIDTH), 0.02),
        "ln_b_g": 1.0 + nrm(ks[8], (L, GMLP_WIDTH), 0.02),
        "ln_b_b": nrm(ks[9], (L, GMLP_WIDTH), 0.02),
        "w_spatial": nrm(ks[10], (L, GMLP_HEADS, GMLP_CHUNK, GMLP_CHUNK), GMLP_CHUNK ** -0.5),
        "b_spatial": 1.0 + nrm(ks[11], (L, GMLP_HEADS, GMLP_CHUNK), 0.02),
        "w_out": nrm(ks[12], (L, MIX_WIDTH, D_MODEL), MIX_WIDTH ** -0.5),
        "ffn_norm_g": 1.0 + nrm(ks[13], (L, D_MODEL), 0.02),
        "w_up": nrm(ks[14], (L, D_MODEL, 2 * D_FF), D_MODEL ** -0.5),
        "conv_f_w": nrm(ks[15], (L, FFN_CONV_K, 2 * D_FF), FFN_CONV_K ** -0.5),
        "conv_f_b": nrm(ks[16], (L, 2 * D_FF), 0.02),
        "w_down": nrm(ks[17], (L, D_FF, D_MODEL), D_FF ** -0.5),
        "final_norm_g": 1.0 + nrm(ks[18], (D_MODEL,), 0.02),
    }


def reference(x, mix_norm_g, w_in, b_in, conv_a_w, conv_a_b, ln_a_g, ln_a_b,
              ln_b_g, ln_b_b, w_spatial, b_spatial, w_out, ffn_norm_g, w_up,
              conv_f_w, conv_f_b, w_down, final_norm_g):
    h = x
    for l in range(DEPTH):
        y = rms_norm(h, mix_norm_g[l])
        z = jnp.einsum("bsd,de->bse", y, w_in[l]) + b_in[l]
        a_val = z[..., :CONV_WIDTH]
        a_gate = z[..., CONV_WIDTH:2 * CONV_WIDTH]
        b_u = z[..., 2 * CONV_WIDTH:2 * CONV_WIDTH + GMLP_WIDTH]
        b_v = z[..., 2 * CONV_WIDTH + GMLP_WIDTH:]
        out_a = conformer_conv_group(a_val, a_gate, conv_a_w[l], conv_a_b[l],
                                     ln_a_g[l], ln_a_b[l])
        out_b = gmlp_spatial_group(b_u, b_v, ln_b_g[l], ln_b_b[l],
                                   w_spatial[l], b_spatial[l])
        mixed = jnp.concatenate([out_a, out_b], axis=-1)
        h = h + jnp.einsum("bse,ed->bsd", mixed, w_out[l])
        y2 = rms_norm(h, ffn_norm_g[l])
        up = jnp.einsum("bsd,df->bsf", y2, w_up[l])
        up = causal_dwconv(up, conv_f_w[l], conv_f_b[l])
        gate, val = up[..., :D_FF], up[..., D_FF:]
        h = h + jnp.einsum("bsf,fd->bsd", jax.nn.silu(gate) * val, w_down[l])
    return rms_norm(h, final_norm_g)
```

```python
import functools
import math

import jax
import jax.numpy as jnp
from jax import lax
from jax.experimental import pallas as pl
from jax.experimental.pallas import tpu as pltpu

F32 = jnp.float32
BF16 = jnp.bfloat16

NORM_EPS = 1e-6
LN_EPS = 1e-5

LANES_V7X = 128
SUBLANES_V7X = 8
VMEM_LIMIT_BYTES_V7X = 56 * 1024 * 1024

GMLP_BLOCK = 128
GMLP_HEAD_DIM = 64
CONV_HIST = 32

TS_MIX = 256
TS_FFN = 256
FFN_CHUNK = 256
CONV_ROWS = 64


def _rms(x, g):
    ms = jnp.mean(x * x, axis=-1, keepdims=True)
    return (x * lax.rsqrt(ms + NORM_EPS)) * g


def _layer_norm(x, g, b):
    mu = jnp.mean(x, axis=-1, keepdims=True)
    xc = x - mu
    var = jnp.mean(xc * xc, axis=-1, keepdims=True)
    return (xc * lax.rsqrt(var + LN_EPS)) * g + b


def _gelu(x):
    return 0.5 * x * (1.0 + lax.erf(x * (1.0 / math.sqrt(2.0))))


def _sigmoid(x):
    return 1.0 / (1.0 + jnp.exp(-x))


def _mixer_kernel(x_ref, ng_ref, win_ref, bin_ref, cw_ref, cb_ref, lag_ref,
                  lab_ref, lbg_ref, lbb_ref, ws_ref, bsp_ref, wout_ref,
                  h_ref, z_ref, abuf_ref, vbuf_ref, mix_ref, *, ts, cw, k_taps):
    t = pl.program_id(1)
    nblk = ts // GMLP_BLOCK

    @pl.when(t == 0)
    def _():
        abuf_ref[0:CONV_HIST, :] = jnp.zeros((CONV_HIST, cw), F32)

    y = _rms(x_ref[0], ng_ref[...])
    z_ref[...] = jnp.dot(y.astype(BF16), win_ref[...],
                         preferred_element_type=F32) + bin_ref[...]

    abuf_ref[CONV_HIST:CONV_HIST + ts, :] = (
        z_ref[:, 0:cw] * _sigmoid(z_ref[:, cw:2 * cw]))
    first = CONV_HIST - (k_taps - 1)
    for c in range(ts // CONV_ROWS):
        r0 = c * CONV_ROWS
        acc = jnp.broadcast_to(cb_ref[...], (CONV_ROWS, cw))
        for k in range(k_taps):
            acc = acc + cw_ref[k:k + 1, :] * abuf_ref[r0 + first + k:r0 + first + k + CONV_ROWS, :]
        a = _layer_norm(acc, lag_ref[...], lab_ref[...])
        mix_ref[r0:r0 + CONV_ROWS, 0:cw] = (a * _sigmoid(a)).astype(BF16)
    abuf_ref[0:CONV_HIST, :] = abuf_ref[ts:ts + CONV_HIST, :]

    row = lax.broadcasted_iota(jnp.int32, (GMLP_BLOCK, 2 * GMLP_BLOCK), 0)
    col = lax.broadcasted_iota(jnp.int32, (GMLP_BLOCK, 2 * GMLP_BLOCK), 1)
    tril = (col % GMLP_BLOCK) <= row
    lane = lax.broadcasted_iota(jnp.int32, (GMLP_BLOCK, LANES_V7X), 1)
    even_head = lane < GMLP_HEAD_DIM
    for c in range(ts // CONV_ROWS):
        rs = slice(c * CONV_ROWS, (c + 1) * CONV_ROWS)
        v = _layer_norm(_gelu(z_ref[rs, 3 * cw:4 * cw]), lbg_ref[...], lbb_ref[...])
        vbuf_ref[rs, :] = v.astype(BF16)
    for j in range(cw // LANES_V7X):
        cs = slice(j * LANES_V7X, (j + 1) * LANES_V7X)
        wj = jnp.where(tril, ws_ref[j], 0.0).astype(BF16)
        rhs = []
        for b in range(nblk):
            rs = slice(b * GMLP_BLOCK, (b + 1) * GMLP_BLOCK)
            vb = vbuf_ref[rs, cs]
            zero = jnp.zeros_like(vb)
            rhs.append(jnp.concatenate(
                [jnp.where(even_head, vb, zero), jnp.where(even_head, zero, vb)], axis=0))
        sp = jnp.dot(wj, jnp.concatenate(rhs, axis=1), preferred_element_type=F32)
        for b in range(nblk):
            rs = slice(b * GMLP_BLOCK, (b + 1) * GMLP_BLOCK)
            u = _gelu(z_ref[rs, 2 * cw + j * LANES_V7X:2 * cw + (j + 1) * LANES_V7X])
            spb = sp[:, b * GMLP_BLOCK:(b + 1) * GMLP_BLOCK] + bsp_ref[:, cs]
            mix_ref[rs, cw + j * LANES_V7X:cw + (j + 1) * LANES_V7X] = (u * spb).astype(BF16)

    h_ref[0] = x_ref[0] + jnp.dot(mix_ref[...], wout_ref[...],
                                  preferred_element_type=F32)


def _ffn_kernel(h_ref, ng_ref, wug_ref, wuv_ref, cwg_ref, cbg_ref, cwv_ref,
                cbv_ref, wd_ref, fg_ref, o_ref, carry_g_ref, carry_v_ref,
                hid_ref, *, ts, fc):
    t = pl.program_id(1)
    n_chunks = wug_ref.shape[0]

    @pl.when(t == 0)
    def _():
        carry_g_ref[...] = jnp.zeros(carry_g_ref.shape, F32)
        carry_v_ref[...] = jnp.zeros(carry_v_ref.shape, F32)

    h = h_ref[0]
    y = _rms(h, ng_ref[...]).astype(BF16)

    def conv3(u, carry_ref, c, w_ref, b_ref):
        cs = slice(c * fc, (c + 1) * fc)
        ext = jnp.concatenate([carry_ref[c], u], axis=0)
        out = (w_ref[0:1, cs] * ext[SUBLANES_V7X - 2:SUBLANES_V7X - 2 + ts]
               + w_ref[1:2, cs] * ext[SUBLANES_V7X - 1:SUBLANES_V7X - 1 + ts]
               + w_ref[2:3, cs] * u + b_ref[:, cs])
        carry_ref[c] = u[ts - SUBLANES_V7X:ts]
        return out

    for c in range(n_chunks):
        ug = jnp.dot(y, wug_ref[c], preferred_element_type=F32)
        uv = jnp.dot(y, wuv_ref[c], preferred_element_type=F32)
        g = conv3(ug, carry_g_ref, c, cwg_ref, cbg_ref)
        v = conv3(uv, carry_v_ref, c, cwv_ref, cbv_ref)
        hid_ref[:, c * fc:(c + 1) * fc] = (g * _sigmoid(g) * v).astype(BF16)

    h2 = h + jnp.dot(hid_ref[...], wd_ref[...], preferred_element_type=F32)
    o_ref[0] = _rms(h2, fg_ref[...])


def _const_spec(shape):
    nd = len(shape)
    return pl.BlockSpec(shape, lambda b, t: (0,) * nd, pipeline_mode=pl.Buffered(1))


def kernel(x, mix_norm_g, w_in, b_in, conv_a_w, conv_a_b, ln_a_g, ln_a_b, ln_b_g, ln_b_b, w_spatial, b_spatial, w_out, ffn_norm_g, w_up, conv_f_w, conv_f_b, w_down, final_norm_g):
    bsz, seq, d = x.shape
    depth = w_in.shape[0]
    assert depth == 1
    in_w = w_in.shape[2]
    cw = in_w // 4
    k_taps = conv_a_w.shape[1]
    heads = w_spatial.shape[1]
    dff = w_down.shape[1]
    assert cw == heads * GMLP_HEAD_DIM and w_spatial.shape[2] == GMLP_BLOCK
    assert k_taps - 1 <= CONV_HIST
    assert seq % TS_MIX == 0 and seq % TS_FFN == 0 and TS_MIX % GMLP_BLOCK == 0
    assert dff % FFN_CHUNK == 0
    n_chunks = dff // FFN_CHUNK

    row = lambda v: v.reshape(1, -1).astype(F32)

    w_in_b = w_in[0].astype(BF16)
    w_out_b = w_out[0].astype(BF16)
    ws_cat = (w_spatial[0].reshape(heads // 2, 2, GMLP_BLOCK, GMLP_BLOCK)
              .transpose(0, 2, 1, 3).reshape(heads // 2, GMLP_BLOCK, 2 * GMLP_BLOCK))
    bsp = jnp.repeat(jnp.transpose(b_spatial[0]), GMLP_HEAD_DIM, axis=1)

    cparams = pltpu.CompilerParams(
        dimension_semantics=("arbitrary", "arbitrary"),
        vmem_limit_bytes=VMEM_LIMIT_BYTES_V7X)

    ts = TS_MIX
    h1 = pl.pallas_call(
        functools.partial(_mixer_kernel, ts=ts, cw=cw, k_taps=k_taps),
        out_shape=jax.ShapeDtypeStruct((bsz, seq, d), F32),
        grid=(bsz, seq // ts),
        in_specs=[
            pl.BlockSpec((1, ts, d), lambda b, t: (b, t, 0)),
            _const_spec((1, d)),
            _const_spec((d, in_w)),
            _const_spec((1, in_w)),
            _const_spec((k_taps, cw)),
            _const_spec((1, cw)),
            _const_spec((1, cw)),
            _const_spec((1, cw)),
            _const_spec((1, cw)),
            _const_spec((1, cw)),
            _const_spec((heads // 2, GMLP_BLOCK, 2 * GMLP_BLOCK)),
            _const_spec((GMLP_BLOCK, cw)),
            _const_spec((2 * cw, d)),
        ],
        out_specs=pl.BlockSpec((1, ts, d), lambda b, t: (b, t, 0)),
        scratch_shapes=[
            pltpu.VMEM((ts, in_w), F32),
            pltpu.VMEM((CONV_HIST + ts, cw), F32),
            pltpu.VMEM((ts, cw), BF16),
            pltpu.VMEM((ts, 2 * cw), BF16),
        ],
        compiler_params=cparams,
        name="mixer",
    )(x, row(mix_norm_g[0]), w_in_b, row(b_in[0]), conv_a_w[0], row(conv_a_b[0]),
      row(ln_a_g[0]), row(ln_a_b[0]), row(ln_b_g[0]), row(ln_b_b[0]), ws_cat, bsp, w_out_b)

    ts = TS_FFN
    fc = FFN_CHUNK
    w_up_b = w_up[0].astype(BF16)
    wug = w_up_b[:, :dff].reshape(d, n_chunks, fc).transpose(1, 0, 2)
    wuv = w_up_b[:, dff:].reshape(d, n_chunks, fc).transpose(1, 0, 2)
    w_down_b = w_down[0].astype(BF16)
    cfw = conv_f_w[0]
    cfb = conv_f_b[0]
    out = pl.pallas_call(
        functools.partial(_ffn_kernel, ts=ts, fc=fc),
        out_shape=jax.ShapeDtypeStruct((bsz, seq, d), F32),
        grid=(bsz, seq // ts),
        in_specs=[
            pl.BlockSpec((1, ts, d), lambda b, t: (b, t, 0)),
            _const_spec((1, d)),
            _const_spec((n_chunks, d, fc)),
            _const_spec((n_chunks, d, fc)),
            _const_spec((cfw.shape[0], dff)),
            _const_spec((1, dff)),
            _const_spec((cfw.shape[0], dff)),
            _const_spec((1, dff)),
            _const_spec((dff, d)),
            _const_spec((1, d)),
        ],
        out_specs=pl.BlockSpec((1, ts, d), lambda b, t: (b, t, 0)),
        scratch_shapes=[
            pltpu.VMEM((n_chunks, SUBLANES_V7X, fc), F32),
            pltpu.VMEM((n_chunks, SUBLANES_V7X, fc), F32),
            pltpu.VMEM((ts, dff), BF16),
        ],
        compiler_params=cparams,
        name="ffn",
    )(h1, row(ffn_norm_g[0]), wug, wuv, cfw[:, :dff], row(cfb[:dff]),
      cfw[:, dff:], row(cfb[dff:]), w_down_b, row(final_norm_g))
    return out
```

```python
import functools
import math

import jax
import jax.numpy as jnp
from jax import lax
from jax.experimental import pallas as pl
from jax.experimental.pallas import tpu as pltpu

F32 = jnp.float32
BF16 = jnp.bfloat16

NORM_EPS = 1e-6
LN_EPS = 1e-5

LANES_V7X = 128
SUBLANES_V7X = 8
MXU_COLS_V7X = 256
VMEM_LIMIT_BYTES_V7X = 56 * 1024 * 1024

GMLP_BLOCK = 128
GMLP_HEAD_DIM = 64
CONV_HIST = 32

TS_MIX = 256
TS_FFN = 512
FFN_CHUNK = 256
CONV_ROWS = 64


def _rms(x, g):
    ms = jnp.mean(x * x, axis=-1, keepdims=True)
    return (x * lax.rsqrt(ms + NORM_EPS)) * g


def _layer_norm(x, g, b):
    mu = jnp.mean(x, axis=-1, keepdims=True)
    xc = x - mu
    var = jnp.mean(xc * xc, axis=-1, keepdims=True)
    return (xc * lax.rsqrt(var + LN_EPS)) * g + b


def _gelu(x):
    return 0.5 * x * (1.0 + lax.erf(x * (1.0 / math.sqrt(2.0))))


def _sigmoid(x):
    return 1.0 / (1.0 + jnp.exp(-x))


def _mixer_kernel(x_ref, ng_ref, win_ref, bin_ref, cw_ref, cb_ref, lag_ref,
                  lab_ref, lbg_ref, lbb_ref, ws_ref, bsp_ref, wout_ref,
                  h_ref, z_ref, abuf_ref, vbuf_ref, mix_ref, *, ts, cw, k_taps):
    t = pl.program_id(1)
    nblk = ts // GMLP_BLOCK
    nbuf = CONV_HIST + ts
    ngrp = cw // LANES_V7X

    @pl.when(t == 0)
    def _():
        abuf_ref[:, 0:CONV_HIST, :] = jnp.zeros((ngrp, CONV_HIST, LANES_V7X), F32)

    y = _rms(x_ref[0], ng_ref[...]).astype(BF16)
    for n0 in range(0, 4 * cw, MXU_COLS_V7X):
        ns = slice(n0, n0 + MXU_COLS_V7X)
        z_ref[:, ns] = jnp.dot(y, win_ref[:, ns], preferred_element_type=F32) + bin_ref[:, ns]

    for j in range(ngrp):
        cs = slice(j * LANES_V7X, (j + 1) * LANES_V7X)
        abuf_ref[j, CONV_HIST:nbuf, :] = (
            z_ref[:, j * LANES_V7X:(j + 1) * LANES_V7X]
            * _sigmoid(z_ref[:, cw + j * LANES_V7X:cw + (j + 1) * LANES_V7X]))
    first = CONV_HIST - (k_taps - 1)
    for c in range(ts // CONV_ROWS):
        r0 = c * CONV_ROWS
        accs = []
        for j in range(ngrp):
            cs = slice(j * LANES_V7X, (j + 1) * LANES_V7X)
            acc = jnp.broadcast_to(cb_ref[:, cs], (CONV_ROWS, LANES_V7X))
            for k in range(k_taps):
                lo = r0 + first + k
                acc = acc + cw_ref[k:k + 1, cs] * abuf_ref[j, lo:lo + CONV_ROWS, :]
            accs.append(acc)
        a = _layer_norm(jnp.concatenate(accs, axis=1), lag_ref[...], lab_ref[...])
        mix_ref[r0:r0 + CONV_ROWS, 0:cw] = (a * _sigmoid(a)).astype(BF16)
    abuf_ref[:, 0:CONV_HIST, :] = abuf_ref[:, ts:nbuf, :]

    row = lax.broadcasted_iota(jnp.int32, (GMLP_BLOCK, 2 * GMLP_BLOCK), 0)
    col = lax.broadcasted_iota(jnp.int32, (GMLP_BLOCK, 2 * GMLP_BLOCK), 1)
    tril = (col % GMLP_BLOCK) <= row
    lane = lax.broadcasted_iota(jnp.int32, (GMLP_BLOCK, LANES_V7X), 1)
    even_head = lane < GMLP_HEAD_DIM
    for c in range(ts // CONV_ROWS):
        rs = slice(c * CONV_ROWS, (c + 1) * CONV_ROWS)
        v = _layer_norm(_gelu(z_ref[rs, 3 * cw:4 * cw]), lbg_ref[...], lbb_ref[...])
        vbuf_ref[rs, :] = v.astype(BF16)
    for j in range(cw // LANES_V7X):
        cs = slice(j * LANES_V7X, (j + 1) * LANES_V7X)
        wj = jnp.where(tril, ws_ref[j], 0.0).astype(BF16)
        rhs = []
        for b in range(nblk):
            rs = slice(b * GMLP_BLOCK, (b + 1) * GMLP_BLOCK)
            vb = vbuf_ref[rs, cs]
            zero = jnp.zeros_like(vb)
            rhs.append(jnp.concatenate(
                [jnp.where(even_head, vb, zero), jnp.where(even_head, zero, vb)], axis=0))
        sp = jnp.dot(wj, jnp.concatenate(rhs, axis=1), preferred_element_type=F32)
        for b in range(nblk):
            rs = slice(b * GMLP_BLOCK, (b + 1) * GMLP_BLOCK)
            u = _gelu(z_ref[rs, 2 * cw + j * LANES_V7X:2 * cw + (j + 1) * LANES_V7X])
            spb = sp[:, b * GMLP_BLOCK:(b + 1) * GMLP_BLOCK] + bsp_ref[:, cs]
            mix_ref[rs, cw + j * LANES_V7X:cw + (j + 1) * LANES_V7X] = (u * spb).astype(BF16)

    mixed = mix_ref[...]
    for n0 in range(0, h_ref.shape[2], MXU_COLS_V7X):
        ns = slice(n0, n0 + MXU_COLS_V7X)
        h_ref[0, :, ns] = x_ref[0, :, ns] + jnp.dot(mixed, wout_ref[:, ns],
                                                    preferred_element_type=F32)


def _ffn_kernel(h_ref, ng_ref, wu_ref, cw_ref, cb_ref, wd_ref, fg_ref, o_ref,
                ubuf_ref, hid_ref, *, ts, fc, dff):
    t = pl.program_id(1)
    n_chunks = dff // fc
    pad = SUBLANES_V7X

    @pl.when(t == 0)
    def _():
        ubuf_ref[:, 0:pad, :] = jnp.zeros((ubuf_ref.shape[0], pad, LANES_V7X), F32)

    h = h_ref[0]
    y = _rms(h, ng_ref[...]).astype(BF16)

    def conv3(u, c0):
        outs = []
        for i in range(fc // LANES_V7X):
            gi = c0 // LANES_V7X + i
            ls = slice(c0 + i * LANES_V7X, c0 + (i + 1) * LANES_V7X)
            ui = u[:, i * LANES_V7X:(i + 1) * LANES_V7X]
            ubuf_ref[gi, pad:pad + ts, :] = ui
            outs.append(cw_ref[0:1, ls] * ubuf_ref[gi, pad - 2:pad - 2 + ts, :]
                        + cw_ref[1:2, ls] * ubuf_ref[gi, pad - 1:pad - 1 + ts, :]
                        + cw_ref[2:3, ls] * ui + cb_ref[:, ls])
            ubuf_ref[gi, 0:pad, :] = ubuf_ref[gi, ts:ts + pad, :]
        return jnp.concatenate(outs, axis=1)

    for c in range(n_chunks):
        gs, vs = c * fc, dff + c * fc
        g = conv3(jnp.dot(y, wu_ref[:, gs:gs + fc], preferred_element_type=F32), gs)
        v = conv3(jnp.dot(y, wu_ref[:, vs:vs + fc], preferred_element_type=F32), vs)
        hid_ref[:, gs:gs + fc] = (g * _sigmoid(g) * v).astype(BF16)
    hid = hid_ref[...]
    h2 = jnp.concatenate(
        [h[:, n0:n0 + MXU_COLS_V7X]
         + jnp.dot(hid, wd_ref[:, n0:n0 + MXU_COLS_V7X], preferred_element_type=F32)
         for n0 in range(0, h.shape[1], MXU_COLS_V7X)], axis=1)
    o_ref[0] = _rms(h2, fg_ref[...])


def _bf16_weight(w):
    w = w.astype(BF16)
    if (w.shape[1] // LANES_V7X) % SUBLANES_V7X == 0:
        w = jnp.pad(w, ((0, 0), (0, LANES_V7X)))
    return w


def _const_spec(shape):
    nd = len(shape)
    return pl.BlockSpec(shape, lambda b, t: (0,) * nd, pipeline_mode=pl.Buffered(1))


def kernel(x, mix_norm_g, w_in, b_in, conv_a_w, conv_a_b, ln_a_g, ln_a_b, ln_b_g, ln_b_b, w_spatial, b_spatial, w_out, ffn_norm_g, w_up, conv_f_w, conv_f_b, w_down, final_norm_g):
    bsz, seq, d = x.shape
    depth = w_in.shape[0]
    assert depth == 1
    in_w = w_in.shape[2]
    cw = in_w // 4
    k_taps = conv_a_w.shape[1]
    heads = w_spatial.shape[1]
    dff = w_down.shape[1]
    assert cw == heads * GMLP_HEAD_DIM and w_spatial.shape[2] == GMLP_BLOCK
    assert k_taps - 1 <= CONV_HIST
    assert seq % TS_MIX == 0 and seq % TS_FFN == 0 and TS_MIX % GMLP_BLOCK == 0
    assert dff % FFN_CHUNK == 0

    row = lambda v: v.reshape(1, -1).astype(F32)

    w_in_b = _bf16_weight(w_in[0])
    w_out_b = _bf16_weight(w_out[0])
    ws_cat = (w_spatial[0].reshape(heads // 2, 2, GMLP_BLOCK, GMLP_BLOCK)
              .transpose(0, 2, 1, 3).reshape(heads // 2, GMLP_BLOCK, 2 * GMLP_BLOCK))
    bsp = jnp.repeat(jnp.transpose(b_spatial[0]), GMLP_HEAD_DIM, axis=1)

    cparams = pltpu.CompilerParams(
        dimension_semantics=("arbitrary", "arbitrary"),
        vmem_limit_bytes=VMEM_LIMIT_BYTES_V7X)

    ts = TS_MIX
    h1 = pl.pallas_call(
        functools.partial(_mixer_kernel, ts=ts, cw=cw, k_taps=k_taps),
        out_shape=jax.ShapeDtypeStruct((bsz, seq, d), F32),
        grid=(bsz, seq // ts),
        in_specs=[
            pl.BlockSpec((1, ts, d), lambda b, t: (b, t, 0)),
            _const_spec((1, d)),
            _const_spec(w_in_b.shape),
            _const_spec((1, in_w)),
            _const_spec((k_taps, cw)),
            _const_spec((1, cw)),
            _const_spec((1, cw)),
            _const_spec((1, cw)),
            _const_spec((1, cw)),
            _const_spec((1, cw)),
            _const_spec((heads // 2, GMLP_BLOCK, 2 * GMLP_BLOCK)),
            _const_spec((GMLP_BLOCK, cw)),
            _const_spec(w_out_b.shape),
        ],
        out_specs=pl.BlockSpec((1, ts, d), lambda b, t: (b, t, 0)),
        scratch_shapes=[
            pltpu.VMEM((ts, in_w), F32),
            pltpu.VMEM((cw // LANES_V7X, CONV_HIST + ts, LANES_V7X), F32),
            pltpu.VMEM((ts, cw), BF16),
            pltpu.VMEM((ts, 2 * cw), BF16),
        ],
        compiler_params=cparams,
        name="mixer",
    )(x, row(mix_norm_g[0]), w_in_b, row(b_in[0]), conv_a_w[0], row(conv_a_b[0]),
      row(ln_a_g[0]), row(ln_a_b[0]), row(ln_b_g[0]), row(ln_b_b[0]), ws_cat, bsp, w_out_b)

    ts = TS_FFN
    fc = FFN_CHUNK
    w_up_b = _bf16_weight(w_up[0])
    w_down_b = _bf16_weight(w_down[0])
    ffn_k = conv_f_w.shape[1]
    assert ffn_k == 3
    out = pl.pallas_call(
        functools.partial(_ffn_kernel, ts=ts, fc=fc, dff=dff),
        out_shape=jax.ShapeDtypeStruct((bsz, seq, d), F32),
        grid=(bsz, seq // ts),
        in_specs=[
            pl.BlockSpec((1, ts, d), lambda b, t: (b, t, 0)),
            _const_spec((1, d)),
            _const_spec(w_up_b.shape),
            _const_spec((ffn_k, 2 * dff)),
            _const_spec((1, 2 * dff)),
            _const_spec(w_down_b.shape),
            _const_spec((1, d)),
        ],
        out_specs=pl.BlockSpec((1, ts, d), lambda b, t: (b, t, 0)),
        scratch_shapes=[
            pltpu.VMEM((2 * dff // LANES_V7X, SUBLANES_V7X + ts, LANES_V7X), F32),
            pltpu.VMEM((ts, dff), BF16),
        ],
        compiler_params=cparams,
        name="ffn",
    )(h1, row(ffn_norm_g[0]), w_up_b, conv_f_w[0], row(conv_f_b[0]), w_down_b,
      row(final_norm_g))
    return out
```

```python
import functools
import math

import jax
import jax.numpy as jnp
from jax import lax
from jax.experimental import pallas as pl
from jax.experimental.pallas import tpu as pltpu

F32 = jnp.float32
BF16 = jnp.bfloat16

NORM_EPS = 1e-6
LN_EPS = 1e-5

LANES_V7X = 128
SUBLANES_V7X = 8
BF16_ROWS_V7X = 16
MXU_COLS_V7X = 256
VMEM_LIMIT_BYTES_V7X = 56 * 1024 * 1024

GMLP_BLOCK = 128
GMLP_HEAD_DIM = 64
CONV_HIST = 32

TS_MIX = 512
SUB_MIX = GMLP_BLOCK
TS_FFN = 512
FFN_CHUNK = 256
CONV_ROWS = 64
WIDE_ROWS = BF16_ROWS_V7X
NARROW_ROWS = 2 * BF16_ROWS_V7X


def _rms(x, g):
    ms = jnp.mean(x * x, axis=-1, keepdims=True)
    return (x * lax.rsqrt(ms + NORM_EPS)) * g


def _layer_norm(x, g, b):
    mu = jnp.mean(x, axis=-1, keepdims=True)
    xc = x - mu
    var = jnp.mean(xc * xc, axis=-1, keepdims=True)
    return (xc * lax.rsqrt(var + LN_EPS)) * g + b


def _gelu(x):
    return 0.5 * x * (1.0 + lax.erf(x * (1.0 / math.sqrt(2.0))))


def _sigmoid(x):
    return 1.0 / (1.0 + jnp.exp2(x * (-1.0 / math.log(2.0))))


def _dot(a, b):
    return jnp.dot(a, b, preferred_element_type=F32)


def _interleave(major, minor):
    done = 0
    for i, task in enumerate(major):
        task()
        want = (i + 1) * len(minor) // len(major)
        for t in minor[done:want]:
            t()
        done = want


def _mixer_kernel(x_ref, ng_ref, win_ref, bin_ref, cw_ref, cb_ref, lag_ref,
                  lab_ref, lbg_ref, lbb_ref, ws_ref, bsp_ref, wout_ref,
                  h_ref, wsp_ref, ysub_ref, z_ref, abuf_ref, cbuf_ref, vbuf_ref, mix_ref,
                  *, ts, cw, k_taps):
    b = pl.program_id(0)
    t = pl.program_id(1)
    ngrp = cw // LANES_V7X
    sub = SUB_MIX
    n_sub = ts // sub
    d = h_ref.shape[2]
    part = functools.partial

    @pl.when(jnp.logical_and(b == 0, t == 0))
    def _():
        row = lax.broadcasted_iota(jnp.int32, (GMLP_BLOCK, 2 * GMLP_BLOCK), 0)
        col = lax.broadcasted_iota(jnp.int32, (GMLP_BLOCK, 2 * GMLP_BLOCK), 1)
        tril = (col % GMLP_BLOCK) <= row
        for j in range(ngrp):
            wsp_ref[j] = jnp.where(tril, ws_ref[j], 0.0).astype(BF16)

    @pl.when(t == 0)
    def _():
        abuf_ref[:, 0:CONV_HIST, :] = jnp.zeros((ngrp, CONV_HIST, LANES_V7X), F32)

    def m_rms(r, c0):
        xs = x_ref[0, r * sub + c0:r * sub + c0 + WIDE_ROWS, :]
        ysub_ref[r % 2, c0:c0 + WIDE_ROWS, :] = _rms(xs, ng_ref[...]).astype(BF16)

    def m_indot(r, n0):
        ns = slice(n0, n0 + MXU_COLS_V7X)
        z_ref[r % 2, :, ns] = _dot(ysub_ref[r % 2], win_ref[:, ns]) + bin_ref[:, ns]

    def m_glu(r, j):
        for c0 in range(0, sub, NARROW_ROWS):
            rs = slice(c0, c0 + NARROW_ROWS)
            a0 = CONV_HIST + r * sub + c0
            abuf_ref[j, a0:a0 + NARROW_ROWS, :] = (
                z_ref[r % 2, rs, j * LANES_V7X:(j + 1) * LANES_V7X]
                * _sigmoid(z_ref[r % 2, rs, cw + j * LANES_V7X:cw + (j + 1) * LANES_V7X]))

    def m_v(r, c0):
        v = _layer_norm(_gelu(z_ref[r % 2, c0:c0 + WIDE_ROWS, 3 * cw:4 * cw]),
                        lbg_ref[...], lbb_ref[...])
        vbuf_ref[r * sub + c0:r * sub + c0 + WIDE_ROWS, :] = v.astype(BF16)

    def m_conv(r, c, j):
        c0 = r * sub + c * CONV_ROWS
        first = CONV_HIST - (k_taps - 1)
        cs = slice(j * LANES_V7X, (j + 1) * LANES_V7X)
        acc = jnp.broadcast_to(cb_ref[:, cs], (CONV_ROWS, LANES_V7X))
        for k in range(k_taps):
            lo = c0 + first + k
            acc = acc + cw_ref[k:k + 1, cs] * abuf_ref[j, lo:lo + CONV_ROWS, :]
        cbuf_ref[c0:c0 + CONV_ROWS, cs] = acc

    def m_ln(r, c1):
        rs = slice(r * sub + c1, r * sub + c1 + WIDE_ROWS)
        a = _layer_norm(cbuf_ref[rs, :], lag_ref[...], lab_ref[...])
        mix_ref[rs, 0:cw] = (a * _sigmoid(a)).astype(BF16)

    def m_sp(r, j):
        rows = slice(r * sub, (r + 1) * sub)
        lane = lax.broadcasted_iota(jnp.int32, (GMLP_BLOCK, LANES_V7X), 1)
        even_head = lane < GMLP_HEAD_DIM
        cs = slice(j * LANES_V7X, (j + 1) * LANES_V7X)
        vb = vbuf_ref[rows, cs]
        zero = jnp.zeros_like(vb)
        rhs = jnp.concatenate(
            [jnp.where(even_head, vb, zero), jnp.where(even_head, zero, vb)], axis=0)
        sp = _dot(wsp_ref[j], rhs) + bsp_ref[:, cs]
        u = _gelu(z_ref[r % 2, :, 2 * cw + j * LANES_V7X:2 * cw + (j + 1) * LANES_V7X])
        mix_ref[rows, cw + j * LANES_V7X:cw + (j + 1) * LANES_V7X] = (u * sp).astype(BF16)

    def m_outdot(r, n0):
        rows = slice(r * sub, (r + 1) * sub)
        ns = slice(n0, n0 + MXU_COLS_V7X)
        h_ref[0, rows, ns] = x_ref[0, rows, ns] + _dot(mix_ref[rows, :], wout_ref[:, ns])

    def rms_tasks(r):
        return [part(m_rms, r, c0) for c0 in range(0, sub, WIDE_ROWS)]

    def indot_tasks(r):
        return [part(m_indot, r, n0) for n0 in range(0, 4 * cw, MXU_COLS_V7X)]

    def outdot_tasks(r):
        return [part(m_outdot, r, n0) for n0 in range(0, d, MXU_COLS_V7X)]

    def vector_tasks(r):
        tasks = [part(m_glu, r, j) for j in range(ngrp)]
        tasks += [part(m_v, r, c0) for c0 in range(0, sub, WIDE_ROWS)]
        for c in range(sub // CONV_ROWS):
            tasks += [part(m_conv, r, c, j) for j in range(ngrp)]
            tasks += [part(m_ln, r, c * CONV_ROWS + c1) for c1 in range(0, CONV_ROWS, WIDE_ROWS)]
        tasks += [part(m_sp, r, j) for j in range(ngrp)]
        return tasks

    for task in rms_tasks(0) + indot_tasks(0):
        task()
    for r in range(n_sub):
        matmuls = []
        if r + 1 < n_sub:
            for task in rms_tasks(r + 1):
                task()
            matmuls += indot_tasks(r + 1)
        if r > 0:
            matmuls += outdot_tasks(r - 1)
        _interleave(vector_tasks(r), matmuls)
    for task in outdot_tasks(n_sub - 1):
        task()
    abuf_ref[:, 0:CONV_HIST, :] = abuf_ref[:, ts:ts + CONV_HIST, :]


def _ffn_kernel(h_ref, ng_ref, wu_ref, cw_ref, cb_ref, wd_ref, fg_ref, o_ref,
                ubuf_ref, hid_ref, *, ts, fc, dff):
    t = pl.program_id(1)
    n_chunks = dff // fc
    pad = SUBLANES_V7X

    @pl.when(t == 0)
    def _():
        ubuf_ref[:, 0:pad, :] = jnp.zeros((ubuf_ref.shape[0], pad, LANES_V7X), F32)

    h = h_ref[0]
    y = _rms(h, ng_ref[...]).astype(BF16)

    def conv3(u, c0):
        outs = []
        for i in range(fc // LANES_V7X):
            gi = c0 // LANES_V7X + i
            ls = slice(c0 + i * LANES_V7X, c0 + (i + 1) * LANES_V7X)
            ui = u[:, i * LANES_V7X:(i + 1) * LANES_V7X]
            ubuf_ref[gi, pad:pad + ts, :] = ui
            outs.append(cw_ref[0:1, ls] * ubuf_ref[gi, pad - 2:pad - 2 + ts, :]
                        + cw_ref[1:2, ls] * ubuf_ref[gi, pad - 1:pad - 1 + ts, :]
                        + cw_ref[2:3, ls] * ui + cb_ref[:, ls])
            ubuf_ref[gi, 0:pad, :] = ubuf_ref[gi, ts:ts + pad, :]
        return jnp.concatenate(outs, axis=1)

    for c in range(n_chunks):
        gs, vs = c * fc, dff + c * fc
        g = conv3(_dot(y, wu_ref[:, gs:gs + fc]), gs)
        v = conv3(_dot(y, wu_ref[:, vs:vs + fc]), vs)
        hid_ref[:, gs:gs + fc] = (g * _sigmoid(g) * v).astype(BF16)
    hid = hid_ref[...]
    h2 = jnp.concatenate(
        [h[:, n0:n0 + MXU_COLS_V7X] + _dot(hid, wd_ref[:, n0:n0 + MXU_COLS_V7X])
         for n0 in range(0, h.shape[1], MXU_COLS_V7X)], axis=1)
    o_ref[0] = _rms(h2, fg_ref[...])


def _bf16_weight(w):
    w = w.astype(BF16)
    if (w.shape[1] // LANES_V7X) % SUBLANES_V7X == 0:
        w = jnp.pad(w, ((0, 0), (0, LANES_V7X)))
    return w


def _const_spec(shape):
    nd = len(shape)
    return pl.BlockSpec(shape, lambda b, t: (0,) * nd, pipeline_mode=pl.Buffered(1))


def kernel(x, mix_norm_g, w_in, b_in, conv_a_w, conv_a_b, ln_a_g, ln_a_b, ln_b_g, ln_b_b, w_spatial, b_spatial, w_out, ffn_norm_g, w_up, conv_f_w, conv_f_b, w_down, final_norm_g):
    bsz, seq, d = x.shape
    assert w_in.shape[0] == 1
    in_w = w_in.shape[2]
    cw = in_w // 4
    k_taps = conv_a_w.shape[1]
    heads = w_spatial.shape[1]
    dff = w_down.shape[1]
    assert cw == heads * GMLP_HEAD_DIM and w_spatial.shape[2] == GMLP_BLOCK
    assert k_taps - 1 <= CONV_HIST
    assert seq % TS_MIX == 0 and seq % TS_FFN == 0 and TS_MIX % SUB_MIX == 0
    assert SUB_MIX % CONV_ROWS == 0 and CONV_ROWS % WIDE_ROWS == 0 and SUB_MIX % NARROW_ROWS == 0
    assert dff % FFN_CHUNK == 0

    row = lambda v: v.reshape(1, -1).astype(F32)

    w_in_b = _bf16_weight(w_in[0])
    w_out_b = _bf16_weight(w_out[0])
    ws_cat = (w_spatial[0].reshape(heads // 2, 2, GMLP_BLOCK, GMLP_BLOCK)
              .transpose(0, 2, 1, 3).reshape(heads // 2, GMLP_BLOCK, 2 * GMLP_BLOCK))
    bsp = jnp.repeat(jnp.transpose(b_spatial[0]), GMLP_HEAD_DIM, axis=1)

    cparams = pltpu.CompilerParams(
        dimension_semantics=("arbitrary", "arbitrary"),
        vmem_limit_bytes=VMEM_LIMIT_BYTES_V7X)

    ts = TS_MIX
    h1 = pl.pallas_call(
        functools.partial(_mixer_kernel, ts=ts, cw=cw, k_taps=k_taps),
        out_shape=jax.ShapeDtypeStruct((bsz, seq, d), F32),
        grid=(bsz, seq // ts),
        in_specs=[
            pl.BlockSpec((1, ts, d), lambda b, t: (b, t, 0)),
            _const_spec((1, d)),
            _const_spec(w_in_b.shape),
            _const_spec((1, in_w)),
            _const_spec((k_taps, cw)),
            _const_spec((1, cw)),
            _const_spec((1, cw)),
            _const_spec((1, cw)),
            _const_spec((1, cw)),
            _const_spec((1, cw)),
            _const_spec((heads // 2, GMLP_BLOCK, 2 * GMLP_BLOCK)),
            _const_spec((GMLP_BLOCK, cw)),
            _const_spec(w_out_b.shape),
        ],
        out_specs=pl.BlockSpec((1, ts, d), lambda b, t: (b, t, 0)),
        scratch_shapes=[
            pltpu.VMEM((heads // 2, GMLP_BLOCK, 2 * GMLP_BLOCK), BF16),
            pltpu.VMEM((2, SUB_MIX, d), BF16),
            pltpu.VMEM((2, SUB_MIX, in_w), F32),
            pltpu.VMEM((cw // LANES_V7X, CONV_HIST + ts, LANES_V7X), F32),
            pltpu.VMEM((ts, cw), F32),
            pltpu.VMEM((ts, cw), BF16),
            pltpu.VMEM((ts, 2 * cw), BF16),
        ],
        compiler_params=cparams,
        name="mixer",
    )(x, row(mix_norm_g[0]), w_in_b, row(b_in[0]), conv_a_w[0], row(conv_a_b[0]),
      row(ln_a_g[0]), row(ln_a_b[0]), row(ln_b_g[0]), row(ln_b_b[0]), ws_cat, bsp, w_out_b)

    ts = TS_FFN
    fc = FFN_CHUNK
    w_up_b = _bf16_weight(w_up[0])
    w_down_b = _bf16_weight(w_down[0])
    ffn_k = conv_f_w.shape[1]
    assert ffn_k == 3
    out = pl.pallas_call(
        functools.partial(_ffn_kernel, ts=ts, fc=fc, dff=dff),
        out_shape=jax.ShapeDtypeStruct((bsz, seq, d), F32),
        grid=(bsz, seq // ts),
        in_specs=[
            pl.BlockSpec((1, ts, d), lambda b, t: (b, t, 0)),
            _const_spec((1, d)),
            _const_spec(w_up_b.shape),
            _const_spec((ffn_k, 2 * dff)),
            _const_spec((1, 2 * dff)),
            _const_spec(w_down_b.shape),
            _const_spec((1, d)),
        ],
        out_specs=pl.BlockSpec((1, ts, d), lambda b, t: (b, t, 0)),
        scratch_shapes=[
            pltpu.VMEM((2 * dff // LANES_V7X, SUBLANES_V7X + ts, LANES_V7X), F32),
            pltpu.VMEM((ts, dff), BF16),
        ],
        compiler_params=cparams,
        name="ffn",
    )(h1, row(ffn_norm_g[0]), w_up_b, conv_f_w[0], row(conv_f_b[0]), w_down_b,
      row(final_norm_g))
    return out
```

```python
import functools
import math

import jax
import jax.numpy as jnp
from jax import lax
from jax.experimental import pallas as pl
from jax.experimental.pallas import tpu as pltpu

F32 = jnp.float32
BF16 = jnp.bfloat16

NORM_EPS = 1e-6
LN_EPS = 1e-5

LANES_V7X = 128
SUBLANES_V7X = 8
BF16_ROWS_V7X = 16
MXU_COLS_V7X = 256
VMEM_LIMIT_BYTES_V7X = 56 * 1024 * 1024

GMLP_BLOCK = 128
GMLP_HEAD_DIM = 64
CONV_HIST = 32
FFN_TAPS = 3

TS_MIX = 1024
SUB_MIX = GMLP_BLOCK
TS_FFN = 1024
FFN_CHUNK = 256
UBUF_SLOTS = 4
CONV_ROWS = 64
WIDE_ROWS = BF16_ROWS_V7X
NARROW_ROWS = 2 * BF16_ROWS_V7X


def _rms(x, g):
    ms = jnp.mean(x * x, axis=-1, keepdims=True)
    return (x * lax.rsqrt(ms + NORM_EPS)) * g


def _layer_norm(x, g, b):
    mu = jnp.mean(x, axis=-1, keepdims=True)
    xc = x - mu
    var = jnp.mean(xc * xc, axis=-1, keepdims=True)
    return (xc * lax.rsqrt(var + LN_EPS)) * g + b


def _gelu(x):
    return 0.5 * x * (1.0 + lax.erf(x * (1.0 / math.sqrt(2.0))))


def _sigmoid(x):
    return 1.0 / (1.0 + jnp.exp2(x * (-1.0 / math.log(2.0))))


def _dot(a, b):
    return jnp.dot(a, b, preferred_element_type=F32)


def _interleave(major, minor):
    done = 0
    for i, task in enumerate(major):
        task()
        want = (i + 1) * len(minor) // len(major)
        for t in minor[done:want]:
            t()
        done = want


def _mixer_kernel(x_ref, ng_ref, win_ref, bin_ref, cw_ref, cb_ref, lag_ref,
                  lab_ref, lbg_ref, lbb_ref, ws_ref, bsp_ref, wout_ref,
                  h_ref, wsp_ref, ysub_ref, z_ref, abuf_ref, cbuf_ref, vbuf_ref, mix_ref,
                  *, ts, cw, k_taps):
    b = pl.program_id(0)
    t = pl.program_id(1)
    ngrp = cw // LANES_V7X
    sub = SUB_MIX
    n_sub = ts // sub
    d = h_ref.shape[2]
    part = functools.partial

    @pl.when(jnp.logical_and(b == 0, t == 0))
    def _():
        row = lax.broadcasted_iota(jnp.int32, (GMLP_BLOCK, 2 * GMLP_BLOCK), 0)
        col = lax.broadcasted_iota(jnp.int32, (GMLP_BLOCK, 2 * GMLP_BLOCK), 1)
        tril = (col % GMLP_BLOCK) <= row
        for j in range(ngrp):
            wsp_ref[j] = jnp.where(tril, ws_ref[j], 0.0).astype(BF16)

    @pl.when(t == 0)
    def _():
        abuf_ref[:, 0:CONV_HIST, :] = jnp.zeros((ngrp, CONV_HIST, LANES_V7X), F32)

    def m_rms(r, c0):
        xs = x_ref[0, r * sub + c0:r * sub + c0 + WIDE_ROWS, :]
        ysub_ref[r % 2, c0:c0 + WIDE_ROWS, :] = _rms(xs, ng_ref[...]).astype(BF16)

    def m_indot(r, n0):
        ns = slice(n0, n0 + MXU_COLS_V7X)
        z_ref[r % 2, :, ns] = _dot(ysub_ref[r % 2], win_ref[:, ns]) + bin_ref[:, ns]

    def m_glu(r, j):
        for c0 in range(0, sub, NARROW_ROWS):
            rs = slice(c0, c0 + NARROW_ROWS)
            a0 = CONV_HIST + r * sub + c0
            abuf_ref[j, a0:a0 + NARROW_ROWS, :] = (
                z_ref[r % 2, rs, j * LANES_V7X:(j + 1) * LANES_V7X]
                * _sigmoid(z_ref[r % 2, rs, cw + j * LANES_V7X:cw + (j + 1) * LANES_V7X]))

    def m_v(r, c0):
        v = _layer_norm(_gelu(z_ref[r % 2, c0:c0 + WIDE_ROWS, 3 * cw:4 * cw]),
                        lbg_ref[...], lbb_ref[...])
        vbuf_ref[r * sub + c0:r * sub + c0 + WIDE_ROWS, :] = v.astype(BF16)

    def m_conv(r, c, j):
        c0 = r * sub + c * CONV_ROWS
        first = CONV_HIST - (k_taps - 1)
        cs = slice(j * LANES_V7X, (j + 1) * LANES_V7X)
        acc = jnp.broadcast_to(cb_ref[:, cs], (CONV_ROWS, LANES_V7X))
        for k in range(k_taps):
            lo = c0 + first + k
            acc = acc + cw_ref[k:k + 1, cs] * abuf_ref[j, lo:lo + CONV_ROWS, :]
        cbuf_ref[c0:c0 + CONV_ROWS, cs] = acc

    def m_ln(r, c1):
        rs = slice(r * sub + c1, r * sub + c1 + WIDE_ROWS)
        a = _layer_norm(cbuf_ref[rs, :], lag_ref[...], lab_ref[...])
        mix_ref[rs, 0:cw] = (a * _sigmoid(a)).astype(BF16)

    def m_sp(r, j):
        rows = slice(r * sub, (r + 1) * sub)
        lane = lax.broadcasted_iota(jnp.int32, (GMLP_BLOCK, LANES_V7X), 1)
        even_head = lane < GMLP_HEAD_DIM
        cs = slice(j * LANES_V7X, (j + 1) * LANES_V7X)
        vb = vbuf_ref[rows, cs]
        zero = jnp.zeros_like(vb)
        rhs = jnp.concatenate(
            [jnp.where(even_head, vb, zero), jnp.where(even_head, zero, vb)], axis=0)
        sp = _dot(wsp_ref[j], rhs) + bsp_ref[:, cs]
        u = _gelu(z_ref[r % 2, :, 2 * cw + j * LANES_V7X:2 * cw + (j + 1) * LANES_V7X])
        mix_ref[rows, cw + j * LANES_V7X:cw + (j + 1) * LANES_V7X] = (u * sp).astype(BF16)

    def m_outdot(r, n0):
        rows = slice(r * sub, (r + 1) * sub)
        ns = slice(n0, n0 + MXU_COLS_V7X)
        h_ref[0, rows, ns] = x_ref[0, rows, ns] + _dot(mix_ref[rows, :], wout_ref[:, ns])

    def rms_tasks(r):
        return [part(m_rms, r, c0) for c0 in range(0, sub, WIDE_ROWS)]

    def indot_tasks(r):
        return [part(m_indot, r, n0) for n0 in range(0, 4 * cw, MXU_COLS_V7X)]

    def outdot_tasks(r):
        return [part(m_outdot, r, n0) for n0 in range(0, d, MXU_COLS_V7X)]

    def vector_tasks(r):
        tasks = [part(m_glu, r, j) for j in range(ngrp)]
        tasks += [part(m_v, r, c0) for c0 in range(0, sub, WIDE_ROWS)]
        for c in range(sub // CONV_ROWS):
            tasks += [part(m_conv, r, c, j) for j in range(ngrp)]
            tasks += [part(m_ln, r, c * CONV_ROWS + c1) for c1 in range(0, CONV_ROWS, WIDE_ROWS)]
        tasks += [part(m_sp, r, j) for j in range(ngrp)]
        return tasks

    for task in rms_tasks(0) + indot_tasks(0):
        task()
    for r in range(n_sub):
        matmuls = []
        if r + 1 < n_sub:
            for task in rms_tasks(r + 1):
                task()
            matmuls += indot_tasks(r + 1)
        if r > 0:
            matmuls += outdot_tasks(r - 1)
        _interleave(vector_tasks(r), matmuls)
    for task in outdot_tasks(n_sub - 1):
        task()
    abuf_ref[:, 0:CONV_HIST, :] = abuf_ref[:, ts:ts + CONV_HIST, :]


def _ffn_kernel(h_ref, ng_ref, wu_ref, cw_ref, cb_ref, wd_ref, fg_ref, o_ref,
                ubuf_ref, carry_ref, hid_ref, *, ts, fc, dff):
    t = pl.program_id(1)
    n_chunks = dff // fc
    fgrp = fc // LANES_V7X
    pad = SUBLANES_V7X

    @pl.when(t == 0)
    def _():
        carry_ref[...] = jnp.zeros(carry_ref.shape, F32)

    h = h_ref[0]
    y = _rms(h, ng_ref[...]).astype(BF16)

    def conv3(u, c0, slot):
        outs = []
        for i in range(fgrp):
            gi = c0 // LANES_V7X + i
            ls = slice(c0 + i * LANES_V7X, c0 + (i + 1) * LANES_V7X)
            ui = u[:, i * LANES_V7X:(i + 1) * LANES_V7X]
            ubuf_ref[slot, i, 0:pad, :] = carry_ref[gi]
            ubuf_ref[slot, i, pad:pad + ts, :] = ui
            outs.append(cw_ref[0:1, ls] * ubuf_ref[slot, i, pad - 2:pad - 2 + ts, :]
                        + cw_ref[1:2, ls] * ubuf_ref[slot, i, pad - 1:pad - 1 + ts, :]
                        + cw_ref[2:3, ls] * ui + cb_ref[:, ls])
            carry_ref[gi] = ui[ts - pad:ts]
        return jnp.concatenate(outs, axis=1)

    for c in range(n_chunks):
        gs, vs = c * fc, dff + c * fc
        g = conv3(_dot(y, wu_ref[:, gs:gs + fc]), gs, (2 * c) % UBUF_SLOTS)
        v = conv3(_dot(y, wu_ref[:, vs:vs + fc]), vs, (2 * c + 1) % UBUF_SLOTS)
        hid_ref[:, gs:gs + fc] = (g * _sigmoid(g) * v).astype(BF16)
    hid = hid_ref[...]
    h2 = jnp.concatenate(
        [h[:, n0:n0 + MXU_COLS_V7X] + _dot(hid, wd_ref[:, n0:n0 + MXU_COLS_V7X])
         for n0 in range(0, h.shape[1], MXU_COLS_V7X)], axis=1)
    o_ref[0] = _rms(h2, fg_ref[...])


def _bf16_weight(w):
    w = w.astype(BF16)
    if (w.shape[1] // LANES_V7X) % SUBLANES_V7X == 0:
        w = jnp.pad(w, ((0, 0), (0, LANES_V7X)))
    return w


def _const_spec(shape):
    nd = len(shape)
    return pl.BlockSpec(shape, lambda b, t: (0,) * nd, pipeline_mode=pl.Buffered(1))


def kernel(x, mix_norm_g, w_in, b_in, conv_a_w, conv_a_b, ln_a_g, ln_a_b, ln_b_g, ln_b_b, w_spatial, b_spatial, w_out, ffn_norm_g, w_up, conv_f_w, conv_f_b, w_down, final_norm_g):
    bsz, seq, d = x.shape
    assert w_in.shape[0] == 1
    in_w = w_in.shape[2]
    cw = in_w // 4
    k_taps = conv_a_w.shape[1]
    heads = w_spatial.shape[1]
    dff = w_down.shape[1]
    assert cw == heads * GMLP_HEAD_DIM and w_spatial.shape[2] == GMLP_BLOCK
    assert k_taps - 1 <= CONV_HIST
    assert seq % TS_MIX == 0 and seq % TS_FFN == 0 and TS_MIX % SUB_MIX == 0
    assert SUB_MIX % CONV_ROWS == 0 and CONV_ROWS % WIDE_ROWS == 0 and SUB_MIX % NARROW_ROWS == 0
    assert dff % FFN_CHUNK == 0 and conv_f_w.shape[1] == FFN_TAPS

    row = lambda v: v.reshape(1, -1).astype(F32)

    w_in_b = _bf16_weight(w_in[0])
    w_out_b = _bf16_weight(w_out[0])
    ws_cat = (w_spatial[0].reshape(heads // 2, 2, GMLP_BLOCK, GMLP_BLOCK)
              .transpose(0, 2, 1, 3).reshape(heads // 2, GMLP_BLOCK, 2 * GMLP_BLOCK))
    bsp = jnp.repeat(jnp.transpose(b_spatial[0]), GMLP_HEAD_DIM, axis=1)

    cparams = pltpu.CompilerParams(
        dimension_semantics=("arbitrary", "arbitrary"),
        vmem_limit_bytes=VMEM_LIMIT_BYTES_V7X)

    ts = TS_MIX
    h1 = pl.pallas_call(
        functools.partial(_mixer_kernel, ts=ts, cw=cw, k_taps=k_taps),
        out_shape=jax.ShapeDtypeStruct((bsz, seq, d), F32),
        grid=(bsz, seq // ts),
        in_specs=[
            pl.BlockSpec((1, ts, d), lambda b, t: (b, t, 0)),
            _const_spec((1, d)),
            _const_spec(w_in_b.shape),
            _const_spec((1, in_w)),
            _const_spec((k_taps, cw)),
            _const_spec((1, cw)),
            _const_spec((1, cw)),
            _const_spec((1, cw)),
            _const_spec((1, cw)),
            _const_spec((1, cw)),
            _const_spec((heads // 2, GMLP_BLOCK, 2 * GMLP_BLOCK)),
            _const_spec((GMLP_BLOCK, cw)),
            _const_spec(w_out_b.shape),
        ],
        out_specs=pl.BlockSpec((1, ts, d), lambda b, t: (b, t, 0)),
        scratch_shapes=[
            pltpu.VMEM((heads // 2, GMLP_BLOCK, 2 * GMLP_BLOCK), BF16),
            pltpu.VMEM((2, SUB_MIX, d), BF16),
            pltpu.VMEM((2, SUB_MIX, in_w), F32),
            pltpu.VMEM((cw // LANES_V7X, CONV_HIST + ts, LANES_V7X), F32),
            pltpu.VMEM((ts, cw), F32),
            pltpu.VMEM((ts, cw), BF16),
            pltpu.VMEM((ts, 2 * cw), BF16),
        ],
        compiler_params=cparams,
        name="mixer",
    )(x, row(mix_norm_g[0]), w_in_b, row(b_in[0]), conv_a_w[0], row(conv_a_b[0]),
      row(ln_a_g[0]), row(ln_a_b[0]), row(ln_b_g[0]), row(ln_b_b[0]), ws_cat, bsp, w_out_b)

    ts = TS_FFN
    fc = FFN_CHUNK
    w_up_b = _bf16_weight(w_up[0])
    w_down_b = _bf16_weight(w_down[0])
    out = pl.pallas_call(
        functools.partial(_ffn_kernel, ts=ts, fc=fc, dff=dff),
        out_shape=jax.ShapeDtypeStruct((bsz, seq, d), F32),
        grid=(bsz, seq // ts),
        in_specs=[
            pl.BlockSpec((1, ts, d), lambda b, t: (b, t, 0)),
            _const_spec((1, d)),
            _const_spec(w_up_b.shape),
            _const_spec((FFN_TAPS, 2 * dff)),
            _const_spec((1, 2 * dff)),
            _const_spec(w_down_b.shape),
            _const_spec((1, d)),
        ],
        out_specs=pl.BlockSpec((1, ts, d), lambda b, t: (b, t, 0)),
        scratch_shapes=[
            pltpu.VMEM((UBUF_SLOTS, fc // LANES_V7X, SUBLANES_V7X + ts, LANES_V7X), F32),
            pltpu.VMEM((2 * dff // LANES_V7X, SUBLANES_V7X, LANES_V7X), F32),
            pltpu.VMEM((ts, dff), BF16),
        ],
        compiler_params=cparams,
        name="ffn",
    )(h1, row(ffn_norm_g[0]), w_up_b, conv_f_w[0], row(conv_f_b[0]), w_down_b,
      row(final_norm_g))
    return out
```

```python
import functools
import math

import jax
import jax.numpy as jnp
from jax import lax
from jax.experimental import pallas as pl
from jax.experimental.pallas import tpu as pltpu

F32 = jnp.float32
BF16 = jnp.bfloat16

NORM_EPS = 1e-6
LN_EPS = 1e-5

LANES_V7X = 128
SUBLANES_V7X = 8
BF16_ROWS_V7X = 16
MXU_COLS_V7X = 256
VMEM_LIMIT_BYTES_V7X = 56 * 1024 * 1024

GMLP_BLOCK = 128
GMLP_HEAD_DIM = 64
CONV_HIST = 32
FFN_TAPS = 3

TS_MIX = 1024
SUB_MIX = 2 * GMLP_BLOCK
TS_FFN = 1024
FFN_CHUNK = 256
UBUF_SLOTS = 4
CONV_ROWS = 64
WIDE_ROWS = BF16_ROWS_V7X
NARROW_ROWS = 2 * BF16_ROWS_V7X


def _rms(x, g):
    ms = jnp.mean(x * x, axis=-1, keepdims=True)
    return (x * lax.rsqrt(ms + NORM_EPS)) * g


def _layer_norm(x, g, b):
    mu = jnp.mean(x, axis=-1, keepdims=True)
    xc = x - mu
    var = jnp.mean(xc * xc, axis=-1, keepdims=True)
    return (xc * lax.rsqrt(var + LN_EPS)) * g + b


def _gelu(x):
    return 0.5 * x * (1.0 + lax.erf(x * (1.0 / math.sqrt(2.0))))


def _sigmoid(x):
    return 1.0 / (1.0 + jnp.exp2(x * (-1.0 / math.log(2.0))))


def _dot(a, b):
    return jnp.dot(a, b, preferred_element_type=F32)


def _interleave(major, minor):
    done = 0
    for i, task in enumerate(major):
        task()
        want = (i + 1) * len(minor) // len(major)
        for t in minor[done:want]:
            t()
        done = want


def _mixer_kernel(x_ref, ng_ref, win_ref, bin_ref, cw_ref, cb_ref, lag_ref,
                  lab_ref, lbg_ref, lbb_ref, ws_ref, bsp_ref, wout_ref,
                  h_ref, wsp_ref, ysub_ref, z_ref, abuf_ref, cbuf_ref, vbuf_ref, mix_ref,
                  *, ts, cw, k_taps):
    b = pl.program_id(0)
    t = pl.program_id(1)
    ngrp = cw // LANES_V7X
    sub = SUB_MIX
    n_sub = ts // sub
    d = h_ref.shape[2]
    part = functools.partial

    @pl.when(jnp.logical_and(b == 0, t == 0))
    def _():
        row = lax.broadcasted_iota(jnp.int32, (GMLP_BLOCK, 2 * GMLP_BLOCK), 0)
        col = lax.broadcasted_iota(jnp.int32, (GMLP_BLOCK, 2 * GMLP_BLOCK), 1)
        tril = (col % GMLP_BLOCK) <= row
        for j in range(ngrp):
            wsp_ref[j] = jnp.where(tril, ws_ref[j], 0.0).astype(BF16)

    @pl.when(t == 0)
    def _():
        abuf_ref[:, 0:CONV_HIST, :] = jnp.zeros((ngrp, CONV_HIST, LANES_V7X), F32)

    def m_rms(r, c0):
        xs = x_ref[0, r * sub + c0:r * sub + c0 + WIDE_ROWS, :]
        ysub_ref[r % 2, c0:c0 + WIDE_ROWS, :] = _rms(xs, ng_ref[...]).astype(BF16)

    def m_indot(r, n0):
        ns = slice(n0, n0 + MXU_COLS_V7X)
        z_ref[r % 2, :, ns] = _dot(ysub_ref[r % 2], win_ref[:, ns]) + bin_ref[:, ns]

    def m_glu(r, j):
        for c0 in range(0, sub, NARROW_ROWS):
            rs = slice(c0, c0 + NARROW_ROWS)
            a0 = CONV_HIST + r * sub + c0
            abuf_ref[j, a0:a0 + NARROW_ROWS, :] = (
                z_ref[r % 2, rs, j * LANES_V7X:(j + 1) * LANES_V7X]
                * _sigmoid(z_ref[r % 2, rs, cw + j * LANES_V7X:cw + (j + 1) * LANES_V7X]))

    def m_v(r, c0):
        v = _layer_norm(_gelu(z_ref[r % 2, c0:c0 + WIDE_ROWS, 3 * cw:4 * cw]),
                        lbg_ref[...], lbb_ref[...])
        vbuf_ref[r * sub + c0:r * sub + c0 + WIDE_ROWS, :] = v.astype(BF16)

    def m_conv(r, c, j):
        c0 = r * sub + c * CONV_ROWS
        first = CONV_HIST - (k_taps - 1)
        cs = slice(j * LANES_V7X, (j + 1) * LANES_V7X)
        acc = jnp.broadcast_to(cb_ref[:, cs], (CONV_ROWS, LANES_V7X))
        for k in range(k_taps):
            lo = c0 + first + k
            acc = acc + cw_ref[k:k + 1, cs] * abuf_ref[j, lo:lo + CONV_ROWS, :]
        cbuf_ref[c0:c0 + CONV_ROWS, cs] = acc

    def m_ln(r, c1):
        rs = slice(r * sub + c1, r * sub + c1 + WIDE_ROWS)
        a = _layer_norm(cbuf_ref[rs, :], lag_ref[...], lab_ref[...])
        mix_ref[rs, 0:cw] = (a * _sigmoid(a)).astype(BF16)

    def m_sp(r, blk, j):
        zrows = slice(blk * GMLP_BLOCK, (blk + 1) * GMLP_BLOCK)
        rows = slice(r * sub + blk * GMLP_BLOCK, r * sub + (blk + 1) * GMLP_BLOCK)
        lane = lax.broadcasted_iota(jnp.int32, (GMLP_BLOCK, LANES_V7X), 1)
        even_head = lane < GMLP_HEAD_DIM
        cs = slice(j * LANES_V7X, (j + 1) * LANES_V7X)
        vb = vbuf_ref[rows, cs]
        zero = jnp.zeros_like(vb)
        rhs = jnp.concatenate(
            [jnp.where(even_head, vb, zero), jnp.where(even_head, zero, vb)], axis=0)
        sp = _dot(wsp_ref[j], rhs) + bsp_ref[:, cs]
        u = _gelu(z_ref[r % 2, zrows, 2 * cw + j * LANES_V7X:2 * cw + (j + 1) * LANES_V7X])
        mix_ref[rows, cw + j * LANES_V7X:cw + (j + 1) * LANES_V7X] = (u * sp).astype(BF16)

    def m_outdot(r, n0):
        rows = slice(r * sub, (r + 1) * sub)
        ns = slice(n0, n0 + MXU_COLS_V7X)
        h_ref[0, rows, ns] = x_ref[0, rows, ns] + _dot(mix_ref[rows, :], wout_ref[:, ns])

    def rms_tasks(r):
        return [part(m_rms, r, c0) for c0 in range(0, sub, WIDE_ROWS)]

    def indot_tasks(r):
        return [part(m_indot, r, n0) for n0 in range(0, 4 * cw, MXU_COLS_V7X)]

    def outdot_tasks(r):
        return [part(m_outdot, r, n0) for n0 in range(0, d, MXU_COLS_V7X)]

    def vector_tasks(r):
        tasks = [part(m_glu, r, j) for j in range(ngrp)]
        tasks += [part(m_v, r, c0) for c0 in range(0, sub, WIDE_ROWS)]
        for c in range(sub // CONV_ROWS):
            tasks += [part(m_conv, r, c, j) for j in range(ngrp)]
            tasks += [part(m_ln, r, c * CONV_ROWS + c1) for c1 in range(0, CONV_ROWS, WIDE_ROWS)]
        tasks += [part(m_sp, r, blk, j) for blk in range(sub // GMLP_BLOCK) for j in range(ngrp)]
        return tasks

    for task in rms_tasks(0) + indot_tasks(0):
        task()
    for r in range(n_sub):
        matmuls = []
        if r + 1 < n_sub:
            for task in rms_tasks(r + 1):
                task()
            matmuls += indot_tasks(r + 1)
        if r > 0:
            matmuls += outdot_tasks(r - 1)
        _interleave(vector_tasks(r), matmuls)
    for task in outdot_tasks(n_sub - 1):
        task()
    abuf_ref[:, 0:CONV_HIST, :] = abuf_ref[:, ts:ts + CONV_HIST, :]


def _ffn_kernel(h_ref, ng_ref, wu_ref, cw_ref, cb_ref, wd_ref, fg_ref, o_ref,
                ubuf_ref, carry_ref, hid_ref, *, ts, fc, dff):
    t = pl.program_id(1)
    n_chunks = dff // fc
    fgrp = fc // LANES_V7X
    pad = SUBLANES_V7X

    @pl.when(t == 0)
    def _():
        carry_ref[...] = jnp.zeros(carry_ref.shape, F32)

    h = h_ref[0]
    y = _rms(h, ng_ref[...]).astype(BF16)

    def conv3(u, c0, slot):
        outs = []
        for i in range(fgrp):
            gi = c0 // LANES_V7X + i
            ls = slice(c0 + i * LANES_V7X, c0 + (i + 1) * LANES_V7X)
            ui = u[:, i * LANES_V7X:(i + 1) * LANES_V7X]
            ubuf_ref[slot, i, 0:pad, :] = carry_ref[gi]
            ubuf_ref[slot, i, pad:pad + ts, :] = ui
            outs.append(cw_ref[0:1, ls] * ubuf_ref[slot, i, pad - 2:pad - 2 + ts, :]
                        + cw_ref[1:2, ls] * ubuf_ref[slot, i, pad - 1:pad - 1 + ts, :]
                        + cw_ref[2:3, ls] * ui + cb_ref[:, ls])
            carry_ref[gi] = ui[ts - pad:ts]
        return jnp.concatenate(outs, axis=1)

    for c in range(n_chunks):
        gs, vs = c * fc, dff + c * fc
        g = conv3(_dot(y, wu_ref[:, gs:gs + fc]), gs, (2 * c) % UBUF_SLOTS)
        v = conv3(_dot(y, wu_ref[:, vs:vs + fc]), vs, (2 * c + 1) % UBUF_SLOTS)
        hid_ref[:, gs:gs + fc] = (g * _sigmoid(g) * v).astype(BF16)
    hid = hid_ref[...]
    h2 = jnp.concatenate(
        [h[:, n0:n0 + MXU_COLS_V7X] + _dot(hid, wd_ref[:, n0:n0 + MXU_COLS_V7X])
         for n0 in range(0, h.shape[1], MXU_COLS_V7X)], axis=1)
    o_ref[0] = _rms(h2, fg_ref[...])


def _bf16_weight(w):
    w = w.astype(BF16)
    if (w.shape[1] // LANES_V7X) % SUBLANES_V7X == 0:
        w = jnp.pad(w, ((0, 0), (0, LANES_V7X)))
    return w


def _const_spec(shape):
    nd = len(shape)
    return pl.BlockSpec(shape, lambda b, t: (0,) * nd, pipeline_mode=pl.Buffered(1))


def kernel(x, mix_norm_g, w_in, b_in, conv_a_w, conv_a_b, ln_a_g, ln_a_b, ln_b_g, ln_b_b, w_spatial, b_spatial, w_out, ffn_norm_g, w_up, conv_f_w, conv_f_b, w_down, final_norm_g):
    bsz, seq, d = x.shape
    assert w_in.shape[0] == 1
    in_w = w_in.shape[2]
    cw = in_w // 4
    k_taps = conv_a_w.shape[1]
    heads = w_spatial.shape[1]
    dff = w_down.shape[1]
    assert cw == heads * GMLP_HEAD_DIM and w_spatial.shape[2] == GMLP_BLOCK
    assert k_taps - 1 <= CONV_HIST
    assert seq % TS_MIX == 0 and seq % TS_FFN == 0 and TS_MIX % SUB_MIX == 0
    assert SUB_MIX % CONV_ROWS == 0 and CONV_ROWS % WIDE_ROWS == 0 and SUB_MIX % NARROW_ROWS == 0
    assert dff % FFN_CHUNK == 0 and conv_f_w.shape[1] == FFN_TAPS

    row = lambda v: v.reshape(1, -1).astype(F32)

    w_in_b = _bf16_weight(w_in[0])
    w_out_b = _bf16_weight(w_out[0])
    ws_cat = (w_spatial[0].reshape(heads // 2, 2, GMLP_BLOCK, GMLP_BLOCK)
              .transpose(0, 2, 1, 3).reshape(heads // 2, GMLP_BLOCK, 2 * GMLP_BLOCK))
    bsp = jnp.repeat(jnp.transpose(b_spatial[0]), GMLP_HEAD_DIM, axis=1)

    cparams = pltpu.CompilerParams(
        dimension_semantics=("arbitrary", "arbitrary"),
        vmem_limit_bytes=VMEM_LIMIT_BYTES_V7X)

    ts = TS_MIX
    h1 = pl.pallas_call(
        functools.partial(_mixer_kernel, ts=ts, cw=cw, k_taps=k_taps),
        out_shape=jax.ShapeDtypeStruct((bsz, seq, d), F32),
        grid=(bsz, seq // ts),
        in_specs=[
            pl.BlockSpec((1, ts, d), lambda b, t: (b, t, 0)),
            _const_spec((1, d)),
            _const_spec(w_in_b.shape),
            _const_spec((1, in_w)),
            _const_spec((k_taps, cw)),
            _const_spec((1, cw)),
            _const_spec((1, cw)),
            _const_spec((1, cw)),
            _const_spec((1, cw)),
            _const_spec((1, cw)),
            _const_spec((heads // 2, GMLP_BLOCK, 2 * GMLP_BLOCK)),
            _const_spec((GMLP_BLOCK, cw)),
            _const_spec(w_out_b.shape),
        ],
        out_specs=pl.BlockSpec((1, ts, d), lambda b, t: (b, t, 0)),
        scratch_shapes=[
            pltpu.VMEM((heads // 2, GMLP_BLOCK, 2 * GMLP_BLOCK), BF16),
            pltpu.VMEM((2, SUB_MIX, d), BF16),
            pltpu.VMEM((2, SUB_MIX, in_w), F32),
            pltpu.VMEM((cw // LANES_V7X, CONV_HIST + ts, LANES_V7X), F32),
            pltpu.VMEM((ts, cw), F32),
            pltpu.VMEM((ts, cw), BF16),
            pltpu.VMEM((ts, 2 * cw), BF16),
        ],
        compiler_params=cparams,
        name="mixer",
    )(x, row(mix_norm_g[0]), w_in_b, row(b_in[0]), conv_a_w[0], row(conv_a_b[0]),
      row(ln_a_g[0]), row(ln_a_b[0]), row(ln_b_g[0]), row(ln_b_b[0]), ws_cat, bsp, w_out_b)

    ts = TS_FFN
    fc = FFN_CHUNK
    w_up_b = _bf16_weight(w_up[0])
    w_down_b = _bf16_weight(w_down[0])
    out = pl.pallas_call(
        functools.partial(_ffn_kernel, ts=ts, fc=fc, dff=dff),
        out_shape=jax.ShapeDtypeStruct((bsz, seq, d), F32),
        grid=(bsz, seq // ts),
        in_specs=[
            pl.BlockSpec((1, ts, d), lambda b, t: (b, t, 0)),
            _const_spec((1, d)),
            _const_spec(w_up_b.shape),
            _const_spec((FFN_TAPS, 2 * dff)),
            _const_spec((1, 2 * dff)),
            _const_spec(w_down_b.shape),
            _const_spec((1, d)),
        ],
        out_specs=pl.BlockSpec((1, ts, d), lambda b, t: (b, t, 0)),
        scratch_shapes=[
            pltpu.VMEM((UBUF_SLOTS, fc // LANES_V7X, SUBLANES_V7X + ts, LANES_V7X), F32),
            pltpu.VMEM((2 * dff // LANES_V7X, SUBLANES_V7X, LANES_V7X), F32),
            pltpu.VMEM((ts, dff), BF16),
        ],
        compiler_params=cparams,
        name="ffn",
    )(h1, row(ffn_norm_g[0]), w_up_b, conv_f_w[0], row(conv_f_b[0]), w_down_b,
      row(final_norm_g))
    return out
```

```python
import functools
import math

import jax
import jax.numpy as jnp
from jax import lax
from jax.experimental import pallas as pl
from jax.experimental.pallas import tpu as pltpu

F32 = jnp.float32
BF16 = jnp.bfloat16

NORM_EPS = 1e-6
LN_EPS = 1e-5

LANES_V7X = 128
SUBLANES_V7X = 8
BF16_ROWS_V7X = 16
MXU_COLS_V7X = 256
VMEM_LIMIT_BYTES_V7X = 56 * 1024 * 1024

GMLP_BLOCK = 128
GMLP_HEAD_DIM = 64
CONV_HIST = 32
FFN_TAPS = 3

TS_MIX = 1024
SUB_MIX = 2 * GMLP_BLOCK
TS_FFN = 1024
FFN_CHUNK = 256
UBUF_SLOTS = 4
CONV_ROWS = 64
WIDE_ROWS = BF16_ROWS_V7X
NARROW_ROWS = 2 * BF16_ROWS_V7X


def _rms(x, g):
    ms = jnp.mean(x * x, axis=-1, keepdims=True)
    return (x * lax.rsqrt(ms + NORM_EPS)) * g


def _layer_norm(x, g, b):
    mu = jnp.mean(x, axis=-1, keepdims=True)
    xc = x - mu
    var = jnp.mean(xc * xc, axis=-1, keepdims=True)
    return (xc * lax.rsqrt(var + LN_EPS)) * g + b


def _gelu(x):
    return 0.5 * x * (1.0 + lax.erf(x * (1.0 / math.sqrt(2.0))))


def _sigmoid(x):
    return 1.0 / (1.0 + jnp.exp2(x * (-1.0 / math.log(2.0))))


def _dot(a, b):
    return jnp.dot(a, b, preferred_element_type=F32)


def _interleave(major, minor):
    done = 0
    for i, task in enumerate(major):
        task()
        want = (i + 1) * len(minor) // len(major)
        for t in minor[done:want]:
            t()
        done = want


def _mixer_kernel(x_ref, ng_ref, win_ref, bin_ref, cw_ref, cb_ref, lag_ref,
                  lab_ref, lbg_ref, lbb_ref, ws_ref, bsp_ref, wout_ref,
                  h_ref, wsp_ref, ysub_ref, z_ref, abuf_ref, cbuf_ref, vbuf_ref, mix_ref,
                  *, ts, cw, k_taps):
    b = pl.program_id(0)
    t = pl.program_id(1)
    ngrp = cw // LANES_V7X
    sub = SUB_MIX
    n_sub = ts // sub
    d = h_ref.shape[2]
    part = functools.partial

    @pl.when(jnp.logical_and(b == 0, t == 0))
    def _():
        row = lax.broadcasted_iota(jnp.int32, (GMLP_BLOCK, 2 * GMLP_BLOCK), 0)
        col = lax.broadcasted_iota(jnp.int32, (GMLP_BLOCK, 2 * GMLP_BLOCK), 1)
        tril = (col % GMLP_BLOCK) <= row
        for j in range(ngrp):
            wsp_ref[j] = jnp.where(tril, ws_ref[j], 0.0).astype(BF16)
        abuf_ref[:, :, ts:ts + CONV_HIST, :] = jnp.zeros(
            (SUBLANES_V7X, ngrp, CONV_HIST, LANES_V7X), F32)

    @pl.when(t == 0)
    def _():
        abuf_ref[:, :, 0:CONV_HIST, :] = jnp.zeros((SUBLANES_V7X, ngrp, CONV_HIST, LANES_V7X), F32)

    def m_rms(r, c0):
        xs = x_ref[0, r * sub + c0:r * sub + c0 + WIDE_ROWS, :]
        ysub_ref[r % 2, c0:c0 + WIDE_ROWS, :] = _rms(xs, ng_ref[...]).astype(BF16)

    def m_indot(r, n0):
        ns = slice(n0, n0 + MXU_COLS_V7X)
        z_ref[r % 2, :, ns] = _dot(ysub_ref[r % 2], win_ref[:, ns]) + bin_ref[:, ns]

    def m_glu(r, j):
        for c0 in range(0, sub, NARROW_ROWS):
            rs = slice(c0, c0 + NARROW_ROWS)
            a0 = CONV_HIST + r * sub + c0
            a = (z_ref[r % 2, rs, j * LANES_V7X:(j + 1) * LANES_V7X]
                 * _sigmoid(z_ref[r % 2, rs, cw + j * LANES_V7X:cw + (j + 1) * LANES_V7X]))
            for p in range(SUBLANES_V7X):
                abuf_ref[p, j, a0 - p:a0 - p + NARROW_ROWS, :] = a

    def m_v(r, c0):
        v = _layer_norm(_gelu(z_ref[r % 2, c0:c0 + WIDE_ROWS, 3 * cw:4 * cw]),
                        lbg_ref[...], lbb_ref[...])
        vbuf_ref[r * sub + c0:r * sub + c0 + WIDE_ROWS, :] = v.astype(BF16)

    def m_conv(r, c, j):
        c0 = r * sub + c * CONV_ROWS
        first = CONV_HIST - (k_taps - 1)
        cs = slice(j * LANES_V7X, (j + 1) * LANES_V7X)
        acc = jnp.broadcast_to(cb_ref[:, cs], (CONV_ROWS, LANES_V7X))
        for k in range(k_taps):
            p = (first + k) % SUBLANES_V7X
            lo = c0 + first + k - p
            acc = acc + cw_ref[k:k + 1, cs] * abuf_ref[p, j, lo:lo + CONV_ROWS, :]
        cbuf_ref[c0:c0 + CONV_ROWS, cs] = acc

    def m_ln(r, c1):
        rs = slice(r * sub + c1, r * sub + c1 + WIDE_ROWS)
        a = _layer_norm(cbuf_ref[rs, :], lag_ref[...], lab_ref[...])
        mix_ref[rs, 0:cw] = (a * _sigmoid(a)).astype(BF16)

    def m_sp(r, blk, j):
        zrows = slice(blk * GMLP_BLOCK, (blk + 1) * GMLP_BLOCK)
        rows = slice(r * sub + blk * GMLP_BLOCK, r * sub + (blk + 1) * GMLP_BLOCK)
        lane = lax.broadcasted_iota(jnp.int32, (GMLP_BLOCK, LANES_V7X), 1)
        even_head = lane < GMLP_HEAD_DIM
        cs = slice(j * LANES_V7X, (j + 1) * LANES_V7X)
        vb = vbuf_ref[rows, cs]
        zero = jnp.zeros_like(vb)
        rhs = jnp.concatenate(
            [jnp.where(even_head, vb, zero), jnp.where(even_head, zero, vb)], axis=0)
        sp = _dot(wsp_ref[j], rhs) + bsp_ref[:, cs]
        u = _gelu(z_ref[r % 2, zrows, 2 * cw + j * LANES_V7X:2 * cw + (j + 1) * LANES_V7X])
        mix_ref[rows, cw + j * LANES_V7X:cw + (j + 1) * LANES_V7X] = (u * sp).astype(BF16)

    def m_outdot(r, n0):
        rows = slice(r * sub, (r + 1) * sub)
        ns = slice(n0, n0 + MXU_COLS_V7X)
        h_ref[0, rows, ns] = x_ref[0, rows, ns] + _dot(mix_ref[rows, :], wout_ref[:, ns])

    def rms_tasks(r):
        return [part(m_rms, r, c0) for c0 in range(0, sub, WIDE_ROWS)]

    def indot_tasks(r):
        return [part(m_indot, r, n0) for n0 in range(0, 4 * cw, MXU_COLS_V7X)]

    def outdot_tasks(r):
        return [part(m_outdot, r, n0) for n0 in range(0, d, MXU_COLS_V7X)]

    def vector_tasks(r):
        tasks = [part(m_glu, r, j) for j in range(ngrp)]
        tasks += [part(m_v, r, c0) for c0 in range(0, sub, WIDE_ROWS)]
        for c in range(sub // CONV_ROWS):
            tasks += [part(m_conv, r, c, j) for j in range(ngrp)]
            tasks += [part(m_ln, r, c * CONV_ROWS + c1) for c1 in range(0, CONV_ROWS, WIDE_ROWS)]
        tasks += [part(m_sp, r, blk, j) for blk in range(sub // GMLP_BLOCK) for j in range(ngrp)]
        return tasks

    for task in rms_tasks(0) + indot_tasks(0):
        task()
    for r in range(n_sub):
        matmuls = []
        if r + 1 < n_sub:
            for task in rms_tasks(r + 1):
                task()
            matmuls += indot_tasks(r + 1)
        if r > 0:
            matmuls += outdot_tasks(r - 1)
        _interleave(vector_tasks(r), matmuls)
    for task in outdot_tasks(n_sub - 1):
        task()
    abuf_ref[:, :, 0:CONV_HIST, :] = abuf_ref[:, :, ts:ts + CONV_HIST, :]


def _ffn_kernel(h_ref, ng_ref, wu_ref, cw_ref, cb_ref, wd_ref, fg_ref, o_ref,
                ubuf_ref, carry_ref, hid_ref, *, ts, fc, dff):
    t = pl.program_id(1)
    n_chunks = dff // fc
    fgrp = fc // LANES_V7X
    pad = SUBLANES_V7X

    @pl.when(t == 0)
    def _():
        carry_ref[...] = jnp.zeros(carry_ref.shape, F32)

    h = h_ref[0]
    y = _rms(h, ng_ref[...]).astype(BF16)

    def conv3(u, c0, slot):
        outs = []
        for i in range(fgrp):
            gi = c0 // LANES_V7X + i
            ls = slice(c0 + i * LANES_V7X, c0 + (i + 1) * LANES_V7X)
            ui = u[:, i * LANES_V7X:(i + 1) * LANES_V7X]
            ubuf_ref[slot, i, 0:pad, :] = carry_ref[gi]
            ubuf_ref[slot, i, pad:pad + ts, :] = ui
            outs.append(cw_ref[0:1, ls] * ubuf_ref[slot, i, pad - 2:pad - 2 + ts, :]
                        + cw_ref[1:2, ls] * ubuf_ref[slot, i, pad - 1:pad - 1 + ts, :]
                        + cw_ref[2:3, ls] * ui + cb_ref[:, ls])
            carry_ref[gi] = ui[ts - pad:ts]
        return jnp.concatenate(outs, axis=1)

    for c in range(n_chunks):
        gs, vs = c * fc, dff + c * fc
        g = conv3(_dot(y, wu_ref[:, gs:gs + fc]), gs, (2 * c) % UBUF_SLOTS)
        v = conv3(_dot(y, wu_ref[:, vs:vs + fc]), vs, (2 * c + 1) % UBUF_SLOTS)
        hid_ref[:, gs:gs + fc] = (g * _sigmoid(g) * v).astype(BF16)
    hid = hid_ref[...]
    h2 = jnp.concatenate(
        [h[:, n0:n0 + MXU_COLS_V7X] + _dot(hid, wd_ref[:, n0:n0 + MXU_COLS_V7X])
         for n0 in range(0, h.shape[1], MXU_COLS_V7X)], axis=1)
    o_ref[0] = _rms(h2, fg_ref[...])


def _bf16_weight(w):
    w = w.astype(BF16)
    if (w.shape[1] // LANES_V7X) % SUBLANES_V7X == 0:
        w = jnp.pad(w, ((0, 0), (0, LANES_V7X)))
    return w


def _const_spec(shape):
    nd = len(shape)
    return pl.BlockSpec(shape, lambda b, t: (0,) * nd, pipeline_mode=pl.Buffered(1))


def kernel(x, mix_norm_g, w_in, b_in, conv_a_w, conv_a_b, ln_a_g, ln_a_b, ln_b_g, ln_b_b, w_spatial, b_spatial, w_out, ffn_norm_g, w_up, conv_f_w, conv_f_b, w_down, final_norm_g):
    bsz, seq, d = x.shape
    assert w_in.shape[0] == 1
    in_w = w_in.shape[2]
    cw = in_w // 4
    k_taps = conv_a_w.shape[1]
    heads = w_spatial.shape[1]
    dff = w_down.shape[1]
    assert cw == heads * GMLP_HEAD_DIM and w_spatial.shape[2] == GMLP_BLOCK
    assert k_taps - 1 <= CONV_HIST
    assert seq % TS_MIX == 0 and seq % TS_FFN == 0 and TS_MIX % SUB_MIX == 0
    assert SUB_MIX % CONV_ROWS == 0 and CONV_ROWS % WIDE_ROWS == 0 and SUB_MIX % NARROW_ROWS == 0
    assert dff % FFN_CHUNK == 0 and conv_f_w.shape[1] == FFN_TAPS

    row = lambda v: v.reshape(1, -1).astype(F32)

    w_in_b = _bf16_weight(w_in[0])
    w_out_b = _bf16_weight(w_out[0])
    ws_cat = (w_spatial[0].reshape(heads // 2, 2, GMLP_BLOCK, GMLP_BLOCK)
              .transpose(0, 2, 1, 3).reshape(heads // 2, GMLP_BLOCK, 2 * GMLP_BLOCK))
    bsp = jnp.repeat(jnp.transpose(b_spatial[0]), GMLP_HEAD_DIM, axis=1)

    cparams = pltpu.CompilerParams(
        dimension_semantics=("arbitrary", "arbitrary"),
        vmem_limit_bytes=VMEM_LIMIT_BYTES_V7X)

    ts = TS_MIX
    h1 = pl.pallas_call(
        functools.partial(_mixer_kernel, ts=ts, cw=cw, k_taps=k_taps),
        out_shape=jax.ShapeDtypeStruct((bsz, seq, d), F32),
        grid=(bsz, seq // ts),
        in_specs=[
            pl.BlockSpec((1, ts, d), lambda b, t: (b, t, 0)),
            _const_spec((1, d)),
            _const_spec(w_in_b.shape),
            _const_spec((1, in_w)),
            _const_spec((k_taps, cw)),
            _const_spec((1, cw)),
            _const_spec((1, cw)),
            _const_spec((1, cw)),
            _const_spec((1, cw)),
            _const_spec((1, cw)),
            _const_spec((heads // 2, GMLP_BLOCK, 2 * GMLP_BLOCK)),
            _const_spec((GMLP_BLOCK, cw)),
            _const_spec(w_out_b.shape),
        ],
        out_specs=pl.BlockSpec((1, ts, d), lambda b, t: (b, t, 0)),
        scratch_shapes=[
            pltpu.VMEM((heads // 2, GMLP_BLOCK, 2 * GMLP_BLOCK), BF16),
            pltpu.VMEM((2, SUB_MIX, d), BF16),
            pltpu.VMEM((2, SUB_MIX, in_w), F32),
            pltpu.VMEM((SUBLANES_V7X, cw // LANES_V7X, CONV_HIST + ts, LANES_V7X), F32),
            pltpu.VMEM((ts, cw), F32),
            pltpu.VMEM((ts, cw), BF16),
            pltpu.VMEM((ts, 2 * cw), BF16),
        ],
        compiler_params=cparams,
        name="mixer",
    )(x, row(mix_norm_g[0]), w_in_b, row(b_in[0]), conv_a_w[0], row(conv_a_b[0]),
      row(ln_a_g[0]), row(ln_a_b[0]), row(ln_b_g[0]), row(ln_b_b[0]), ws_cat, bsp, w_out_b)

    ts = TS_FFN
    fc = FFN_CHUNK
    w_up_b = _bf16_weight(w_up[0])
    w_down_b = _bf16_weight(w_down[0])
    out = pl.pallas_call(
        functools.partial(_ffn_kernel, ts=ts, fc=fc, dff=dff),
        out_shape=jax.ShapeDtypeStruct((bsz, seq, d), F32),
        grid=(bsz, seq // ts),
        in_specs=[
            pl.BlockSpec((1, ts, d), lambda b, t: (b, t, 0)),
            _const_spec((1, d)),
            _const_spec(w_up_b.shape),
            _const_spec((FFN_TAPS, 2 * dff)),
            _const_spec((1, 2 * dff)),
            _const_spec(w_down_b.shape),
            _const_spec((1, d)),
        ],
        out_specs=pl.BlockSpec((1, ts, d), lambda b, t: (b, t, 0)),
        scratch_shapes=[
            pltpu.VMEM((UBUF_SLOTS, fc // LANES_V7X, SUBLANES_V7X + ts, LANES_V7X), F32),
            pltpu.VMEM((2 * dff // LANES_V7X, SUBLANES_V7X, LANES_V7X), F32),
            pltpu.VMEM((ts, dff), BF16),
        ],
        compiler_params=cparams,
        name="ffn",
    )(h1, row(ffn_norm_g[0]), w_up_b, conv_f_w[0], row(conv_f_b[0]), w_down_b,
      row(final_norm_g))
    return out
```

```python
import functools
import math

import jax
import jax.numpy as jnp
from jax import lax
from jax.experimental import pallas as pl
from jax.experimental.pallas import tpu as pltpu

F32 = jnp.float32
BF16 = jnp.bfloat16

NORM_EPS = 1e-6
LN_EPS = 1e-5

LANES_V7X = 128
SUBLANES_V7X = 8
BF16_ROWS_V7X = 16
MXU_COLS_V7X = 256
VMEM_LIMIT_BYTES_V7X = 56 * 1024 * 1024

GMLP_BLOCK = 128
GMLP_HEAD_DIM = 64
CONV_HIST = 32
FFN_TAPS = 3

TS_MIX = 1024
SUB_MIX = 2 * GMLP_BLOCK
TS_FFN = 1024
FFN_CHUNK = 256
UBUF_SLOTS = 4
CONV_ROWS = 64
WIDE_ROWS = BF16_ROWS_V7X
NARROW_ROWS = 2 * BF16_ROWS_V7X


def _rms(x, g):
    ms = jnp.mean(x * x, axis=-1, keepdims=True)
    return (x * lax.rsqrt(ms + NORM_EPS)) * g


def _layer_norm(x, g, b):
    mu = jnp.mean(x, axis=-1, keepdims=True)
    xc = x - mu
    var = jnp.mean(xc * xc, axis=-1, keepdims=True)
    return (xc * lax.rsqrt(var + LN_EPS)) * g + b


def _gelu(x):
    return 0.5 * x * (1.0 + lax.erf(x * (1.0 / math.sqrt(2.0))))


def _sigmoid(x):
    return 1.0 / (1.0 + jnp.exp2(x * (-1.0 / math.log(2.0))))


def _dot(a, b):
    return jnp.dot(a, b, preferred_element_type=F32)


def _interleave(major, minor):
    done = 0
    for i, task in enumerate(major):
        task()
        want = (i + 1) * len(minor) // len(major)
        for t in minor[done:want]:
            t()
        done = want


def _mixer_kernel(x_ref, ng_ref, win_ref, bin_ref, cw_ref, cb_ref, lag_ref,
                  lab_ref, lbg_ref, lbb_ref, ws_ref, bsp_ref, wout_ref,
                  h_ref, wsp_ref, ysub_ref, z_ref, abuf_ref, cbuf_ref, vbuf_ref, mix_ref,
                  *, ts, cw, k_taps):
    b = pl.program_id(0)
    t = pl.program_id(1)
    ngrp = cw // LANES_V7X
    sub = SUB_MIX
    n_sub = ts // sub
    d = h_ref.shape[2]
    part = functools.partial

    @pl.when(jnp.logical_and(b == 0, t == 0))
    def _():
        row = lax.broadcasted_iota(jnp.int32, (GMLP_BLOCK, 2 * GMLP_BLOCK), 0)
        col = lax.broadcasted_iota(jnp.int32, (GMLP_BLOCK, 2 * GMLP_BLOCK), 1)
        tril = (col % GMLP_BLOCK) <= row
        for j in range(ngrp):
            wsp_ref[j] = jnp.where(tril, ws_ref[j], 0.0).astype(BF16)

    @pl.when(t == 0)
    def _():
        abuf_ref[:, :, 0:CONV_HIST, :] = jnp.zeros((SUBLANES_V7X, ngrp, CONV_HIST, LANES_V7X), F32)

    def m_rms(r, c0):
        xs = x_ref[0, r * sub + c0:r * sub + c0 + WIDE_ROWS, :]
        ysub_ref[r % 2, c0:c0 + WIDE_ROWS, :] = _rms(xs, ng_ref[...]).astype(BF16)

    def m_indot(r, n0):
        ns = slice(n0, n0 + MXU_COLS_V7X)
        z_ref[r % 2, :, ns] = _dot(ysub_ref[r % 2], win_ref[:, ns]) + bin_ref[:, ns]

    def m_glu(r, j):
        for c0 in range(0, sub, NARROW_ROWS):
            rs = slice(c0, c0 + NARROW_ROWS)
            a0 = CONV_HIST + r * sub + c0
            abuf_ref[0, j, a0:a0 + NARROW_ROWS, :] = (
                z_ref[r % 2, rs, j * LANES_V7X:(j + 1) * LANES_V7X]
                * _sigmoid(z_ref[r % 2, rs, cw + j * LANES_V7X:cw + (j + 1) * LANES_V7X]))

    def m_phase(r, j, p):
        lo = CONV_HIST - SUBLANES_V7X + r * sub
        for c0 in range(lo, lo + sub, NARROW_ROWS):
            abuf_ref[p, j, c0:c0 + NARROW_ROWS, :] = abuf_ref[0, j, c0 + p:c0 + p + NARROW_ROWS, :]

    def m_v(r, c0):
        v = _layer_norm(_gelu(z_ref[r % 2, c0:c0 + WIDE_ROWS, 3 * cw:4 * cw]),
                        lbg_ref[...], lbb_ref[...])
        vbuf_ref[r * sub + c0:r * sub + c0 + WIDE_ROWS, :] = v.astype(BF16)

    def m_conv(r, c, j):
        c0 = r * sub + c * CONV_ROWS
        first = CONV_HIST - (k_taps - 1)
        cs = slice(j * LANES_V7X, (j + 1) * LANES_V7X)
        acc = jnp.broadcast_to(cb_ref[:, cs], (CONV_ROWS, LANES_V7X))
        for k in range(k_taps):
            p = (first + k) % SUBLANES_V7X
            lo = c0 + first + k - p
            acc = acc + cw_ref[k:k + 1, cs] * abuf_ref[p, j, lo:lo + CONV_ROWS, :]
        cbuf_ref[c0:c0 + CONV_ROWS, cs] = acc

    def m_ln(r, c1):
        rs = slice(r * sub + c1, r * sub + c1 + WIDE_ROWS)
        a = _layer_norm(cbuf_ref[rs, :], lag_ref[...], lab_ref[...])
        mix_ref[rs, 0:cw] = (a * _sigmoid(a)).astype(BF16)

    def m_sp(r, blk, j):
        zrows = slice(blk * GMLP_BLOCK, (blk + 1) * GMLP_BLOCK)
        rows = slice(r * sub + blk * GMLP_BLOCK, r * sub + (blk + 1) * GMLP_BLOCK)
        lane = lax.broadcasted_iota(jnp.int32, (GMLP_BLOCK, LANES_V7X), 1)
        even_head = lane < GMLP_HEAD_DIM
        cs = slice(j * LANES_V7X, (j + 1) * LANES_V7X)
        vb = vbuf_ref[rows, cs]
        zero = jnp.zeros_like(vb)
        rhs = jnp.concatenate(
            [jnp.where(even_head, vb, zero), jnp.where(even_head, zero, vb)], axis=0)
        sp = _dot(wsp_ref[j], rhs) + bsp_ref[:, cs]
        u = _gelu(z_ref[r % 2, zrows, 2 * cw + j * LANES_V7X:2 * cw + (j + 1) * LANES_V7X])
        mix_ref[rows, cw + j * LANES_V7X:cw + (j + 1) * LANES_V7X] = (u * sp).astype(BF16)

    def m_outdot(r, n0):
        rows = slice(r * sub, (r + 1) * sub)
        ns = slice(n0, n0 + MXU_COLS_V7X)
        h_ref[0, rows, ns] = x_ref[0, rows, ns] + _dot(mix_ref[rows, :], wout_ref[:, ns])

    def rms_tasks(r):
        return [part(m_rms, r, c0) for c0 in range(0, sub, WIDE_ROWS)]

    def indot_tasks(r):
        return [part(m_indot, r, n0) for n0 in range(0, 4 * cw, MXU_COLS_V7X)]

    def outdot_tasks(r):
        return [part(m_outdot, r, n0) for n0 in range(0, d, MXU_COLS_V7X)]

    def vector_tasks(r):
        tasks = []
        for j in range(ngrp):
            tasks.append(part(m_glu, r, j))
            tasks += [part(m_phase, r, j, p) for p in range(1, SUBLANES_V7X)]
        for c in range(sub // CONV_ROWS):
            tasks += [part(m_conv, r, c, j) for j in range(ngrp)]
            tasks += [part(m_ln, r, c * CONV_ROWS + c1) for c1 in range(0, CONV_ROWS, WIDE_ROWS)]
        tasks += [part(m_v, r, c0) for c0 in range(0, sub, WIDE_ROWS)]
        tasks += [part(m_sp, r, blk, j) for blk in range(sub // GMLP_BLOCK) for j in range(ngrp)]
        return tasks

    first_in = indot_tasks(0)
    for task in rms_tasks(0) + first_in[:len(first_in) // 2]:
        task()
    for r in range(n_sub):
        matmuls = first_in[len(first_in) // 2:] if r == 0 else []
        if r + 1 < n_sub:
            for task in rms_tasks(r + 1):
                task()
            matmuls += indot_tasks(r + 1)
        if r > 0:
            matmuls += outdot_tasks(r - 1)
        _interleave(vector_tasks(r), matmuls)
    for task in outdot_tasks(n_sub - 1):
        task()
    abuf_ref[0, :, 0:CONV_HIST, :] = abuf_ref[0, :, ts:ts + CONV_HIST, :]
    filled = CONV_HIST - SUBLANES_V7X
    abuf_ref[1:, :, 0:filled, :] = abuf_ref[1:, :, ts:ts + filled, :]


def _ffn_kernel(h_ref, ng_ref, wu_ref, cw_ref, cb_ref, wd_ref, fg_ref, o_ref,
                ubuf_ref, carry_ref, hid_ref, *, ts, fc, dff):
    t = pl.program_id(1)
    n_chunks = dff // fc
    fgrp = fc // LANES_V7X
    pad = SUBLANES_V7X

    @pl.when(t == 0)
    def _():
        carry_ref[...] = jnp.zeros(carry_ref.shape, F32)

    h = h_ref[0]
    y = _rms(h, ng_ref[...]).astype(BF16)

    def conv3(u, c0, slot):
        outs = []
        for i in range(fgrp):
            gi = c0 // LANES_V7X + i
            ls = slice(c0 + i * LANES_V7X, c0 + (i + 1) * LANES_V7X)
            ui = u[:, i * LANES_V7X:(i + 1) * LANES_V7X]
            ubuf_ref[slot, i, 0:pad, :] = carry_ref[gi]
            ubuf_ref[slot, i, pad:pad + ts, :] = ui
            outs.append(cw_ref[0:1, ls] * ubuf_ref[slot, i, pad - 2:pad - 2 + ts, :]
                        + cw_ref[1:2, ls] * ubuf_ref[slot, i, pad - 1:pad - 1 + ts, :]
                        + cw_ref[2:3, ls] * ui + cb_ref[:, ls])
            carry_ref[gi] = ui[ts - pad:ts]
        return jnp.concatenate(outs, axis=1)

    for c in range(n_chunks):
        gs, vs = c * fc, dff + c * fc
        g = conv3(_dot(y, wu_ref[:, gs:gs + fc]), gs, (2 * c) % UBUF_SLOTS)
        v = conv3(_dot(y, wu_ref[:, vs:vs + fc]), vs, (2 * c + 1) % UBUF_SLOTS)
        hid_ref[:, gs:gs + fc] = (g * _sigmoid(g) * v).astype(BF16)
    hid = hid_ref[...]
    h2 = jnp.concatenate(
        [h[:, n0:n0 + MXU_COLS_V7X] + _dot(hid, wd_ref[:, n0:n0 + MXU_COLS_V7X])
         for n0 in range(0, h.shape[1], MXU_COLS_V7X)], axis=1)
    o_ref[0] = _rms(h2, fg_ref[...])


def _bf16_weight(w):
    w = w.astype(BF16)
    if (w.shape[1] // LANES_V7X) % SUBLANES_V7X == 0:
        w = jnp.pad(w, ((0, 0), (0, LANES_V7X)))
    return w


def _const_spec(shape):
    nd = len(shape)
    return pl.BlockSpec(shape, lambda b, t: (0,) * nd, pipeline_mode=pl.Buffered(1))


def kernel(x, mix_norm_g, w_in, b_in, conv_a_w, conv_a_b, ln_a_g, ln_a_b, ln_b_g, ln_b_b, w_spatial, b_spatial, w_out, ffn_norm_g, w_up, conv_f_w, conv_f_b, w_down, final_norm_g):
    bsz, seq, d = x.shape
    assert w_in.shape[0] == 1
    in_w = w_in.shape[2]
    cw = in_w // 4
    k_taps = conv_a_w.shape[1]
    heads = w_spatial.shape[1]
    dff = w_down.shape[1]
    assert cw == heads * GMLP_HEAD_DIM and w_spatial.shape[2] == GMLP_BLOCK
    assert k_taps - 1 <= CONV_HIST
    assert seq % TS_MIX == 0 and seq % TS_FFN == 0 and TS_MIX % SUB_MIX == 0
    assert SUB_MIX % CONV_ROWS == 0 and CONV_ROWS % WIDE_ROWS == 0 and SUB_MIX % NARROW_ROWS == 0
    assert dff % FFN_CHUNK == 0 and conv_f_w.shape[1] == FFN_TAPS

    row = lambda v: v.reshape(1, -1).astype(F32)

    w_in_b = _bf16_weight(w_in[0])
    w_out_b = _bf16_weight(w_out[0])
    ws_cat = (w_spatial[0].reshape(heads // 2, 2, GMLP_BLOCK, GMLP_BLOCK)
              .transpose(0, 2, 1, 3).reshape(heads // 2, GMLP_BLOCK, 2 * GMLP_BLOCK))
    bsp = jnp.repeat(jnp.transpose(b_spatial[0]), GMLP_HEAD_DIM, axis=1)

    cparams = pltpu.CompilerParams(
        dimension_semantics=("arbitrary", "arbitrary"),
        vmem_limit_bytes=VMEM_LIMIT_BYTES_V7X)

    ts = TS_MIX
    h1 = pl.pallas_call(
        functools.partial(_mixer_kernel, ts=ts, cw=cw, k_taps=k_taps),
        out_shape=jax.ShapeDtypeStruct((bsz, seq, d), F32),
        grid=(bsz, seq // ts),
        in_specs=[
            pl.BlockSpec((1, ts, d), lambda b, t: (b, t, 0)),
            _const_spec((1, d)),
            _const_spec(w_in_b.shape),
            _const_spec((1, in_w)),
            _const_spec((k_taps, cw)),
            _const_spec((1, cw)),
            _const_spec((1, cw)),
            _const_spec((1, cw)),
            _const_spec((1, cw)),
            _const_spec((1, cw)),
            _const_spec((heads // 2, GMLP_BLOCK, 2 * GMLP_BLOCK)),
            _const_spec((GMLP_BLOCK, cw)),
            _const_spec(w_out_b.shape),
        ],
        out_specs=pl.BlockSpec((1, ts, d), lambda b, t: (b, t, 0)),
        scratch_shapes=[
            pltpu.VMEM((heads // 2, GMLP_BLOCK, 2 * GMLP_BLOCK), BF16),
            pltpu.VMEM((2, SUB_MIX, d), BF16),
            pltpu.VMEM((2, SUB_MIX, in_w), F32),
            pltpu.VMEM((SUBLANES_V7X, cw // LANES_V7X, CONV_HIST + ts, LANES_V7X), F32),
            pltpu.VMEM((ts, cw), F32),
            pltpu.VMEM((ts, cw), BF16),
            pltpu.VMEM((ts, 2 * cw), BF16),
        ],
        compiler_params=cparams,
        name="mixer",
    )(x, row(mix_norm_g[0]), w_in_b, row(b_in[0]), conv_a_w[0], row(conv_a_b[0]),
      row(ln_a_g[0]), row(ln_a_b[0]), row(ln_b_g[0]), row(ln_b_b[0]), ws_cat, bsp, w_out_b)

    ts = TS_FFN
    fc = FFN_CHUNK
    w_up_b = _bf16_weight(w_up[0])
    w_down_b = _bf16_weight(w_down[0])
    out = pl.pallas_call(
        functools.partial(_ffn_kernel, ts=ts, fc=fc, dff=dff),
        out_shape=jax.ShapeDtypeStruct((bsz, seq, d), F32),
        grid=(bsz, seq // ts),
        in_specs=[
            pl.BlockSpec((1, ts, d), lambda b, t: (b, t, 0)),
            _const_spec((1, d)),
            _const_spec(w_up_b.shape),
            _const_spec((FFN_TAPS, 2 * dff)),
            _const_spec((1, 2 * dff)),
            _const_spec(w_down_b.shape),
            _const_spec((1, d)),
        ],
        out_specs=pl.BlockSpec((1, ts, d), lambda b, t: (b, t, 0)),
        scratch_shapes=[
            pltpu.VMEM((UBUF_SLOTS, fc // LANES_V7X, SUBLANES_V7X + ts, LANES_V7X), F32),
            pltpu.VMEM((2 * dff // LANES_V7X, SUBLANES_V7X, LANES_V7X), F32),
            pltpu.VMEM((ts, dff), BF16),
        ],
        compiler_params=cparams,
        name="ffn",
    )(h1, row(ffn_norm_g[0]), w_up_b, conv_f_w[0], row(conv_f_b[0]), w_down_b,
      row(final_norm_g))
    return out
```

```python
import functools
import math

import jax
import jax.numpy as jnp
from jax import lax
from jax.experimental import pallas as pl
from jax.experimental.pallas import tpu as pltpu

F32 = jnp.float32
BF16 = jnp.bfloat16

NORM_EPS = 1e-6
LN_EPS = 1e-5

LANES_V7X = 128
SUBLANES_V7X = 8
BF16_ROWS_V7X = 16
MXU_COLS_V7X = 256
VMEM_LIMIT_BYTES_V7X = 56 * 1024 * 1024

GMLP_BLOCK = 128
GMLP_HEAD_DIM = 64
CONV_HIST = 32
FFN_TAPS = 3

TS_MIX = 1024
SUB_MIX = 2 * GMLP_BLOCK
TS_FFN = 1024
FFN_CHUNK = 256
UBUF_SLOTS = 4
CONV_ROWS = 64
WIDE_ROWS = BF16_ROWS_V7X
NARROW_ROWS = 2 * BF16_ROWS_V7X


def _rms(x, g):
    ms = jnp.mean(x * x, axis=-1, keepdims=True)
    return (x * lax.rsqrt(ms + NORM_EPS)) * g


def _layer_norm(x, g, b):
    mu = jnp.mean(x, axis=-1, keepdims=True)
    xc = x - mu
    var = jnp.mean(xc * xc, axis=-1, keepdims=True)
    return (xc * lax.rsqrt(var + LN_EPS)) * g + b


def _gelu(x):
    return 0.5 * x * (1.0 + lax.erf(x * (1.0 / math.sqrt(2.0))))


def _sigmoid(x):
    return 1.0 / (1.0 + jnp.exp2(x * (-1.0 / math.log(2.0))))


def _dot(a, b):
    return jnp.dot(a, b, preferred_element_type=F32)


def _interleave(major, minor):
    done = 0
    for i, task in enumerate(major):
        task()
        want = (i + 1) * len(minor) // len(major)
        for t in minor[done:want]:
            t()
        done = want


def _mixer_kernel(x_ref, ng_ref, win_ref, bin_ref, cw_ref, cb_ref, lag_ref,
                  lab_ref, lbg_ref, lbb_ref, ws_ref, bsp_ref, wout_ref,
                  h_ref, wsp_ref, ysub_ref, z_ref, abuf_ref, cbuf_ref, vbuf_ref, mix_ref,
                  *, ts, cw, k_taps):
    b = pl.program_id(0)
    t = pl.program_id(1)
    ngrp = cw // LANES_V7X
    sub = SUB_MIX
    n_sub = ts // sub
    d = h_ref.shape[2]
    part = functools.partial

    @pl.when(jnp.logical_and(b == 0, t == 0))
    def _():
        row = lax.broadcasted_iota(jnp.int32, (GMLP_BLOCK, 2 * GMLP_BLOCK), 0)
        col = lax.broadcasted_iota(jnp.int32, (GMLP_BLOCK, 2 * GMLP_BLOCK), 1)
        tril = (col % GMLP_BLOCK) <= row
        for j in range(ngrp):
            wsp_ref[j] = jnp.where(tril, ws_ref[j], 0.0).astype(BF16)

    @pl.when(t == 0)
    def _():
        abuf_ref[:, 0:CONV_HIST, :] = jnp.zeros((ngrp, CONV_HIST, LANES_V7X), F32)

    def m_rms(r, c0):
        xs = x_ref[0, r * sub + c0:r * sub + c0 + WIDE_ROWS, :]
        ysub_ref[r % 2, c0:c0 + WIDE_ROWS, :] = _rms(xs, ng_ref[...]).astype(BF16)

    def m_indot(r, n0):
        ns = slice(n0, n0 + MXU_COLS_V7X)
        z_ref[r % 2, :, ns] = _dot(ysub_ref[r % 2], win_ref[:, ns]) + bin_ref[:, ns]

    def m_glu(r, j):
        for c0 in range(0, sub, NARROW_ROWS):
            rs = slice(c0, c0 + NARROW_ROWS)
            a0 = CONV_HIST + r * sub + c0
            abuf_ref[j, a0:a0 + NARROW_ROWS, :] = (
                z_ref[r % 2, rs, j * LANES_V7X:(j + 1) * LANES_V7X]
                * _sigmoid(z_ref[r % 2, rs, cw + j * LANES_V7X:cw + (j + 1) * LANES_V7X]))

    def m_v(r, c0):
        v = _layer_norm(_gelu(z_ref[r % 2, c0:c0 + WIDE_ROWS, 3 * cw:4 * cw]),
                        lbg_ref[...], lbb_ref[...])
        vbuf_ref[r * sub + c0:r * sub + c0 + WIDE_ROWS, :] = v.astype(BF16)

    def m_conv(r, c, j):
        c0 = r * sub + c * CONV_ROWS
        first = CONV_HIST - (k_taps - 1)
        cs = slice(j * LANES_V7X, (j + 1) * LANES_V7X)
        acc = jnp.broadcast_to(cb_ref[:, cs], (CONV_ROWS, LANES_V7X))
        for k in range(k_taps):
            lo = c0 + first + k
            acc = acc + cw_ref[k:k + 1, cs] * abuf_ref[j, lo:lo + CONV_ROWS, :]
        cbuf_ref[c0:c0 + CONV_ROWS, cs] = acc

    def m_ln(r, c1):
        rs = slice(r * sub + c1, r * sub + c1 + WIDE_ROWS)
        a = _layer_norm(cbuf_ref[rs, :], lag_ref[...], lab_ref[...])
        mix_ref[rs, 0:cw] = (a * _sigmoid(a)).astype(BF16)

    def m_sp(r, blk, j):
        zrows = slice(blk * GMLP_BLOCK, (blk + 1) * GMLP_BLOCK)
        rows = slice(r * sub + blk * GMLP_BLOCK, r * sub + (blk + 1) * GMLP_BLOCK)
        lane = lax.broadcasted_iota(jnp.int32, (GMLP_BLOCK, LANES_V7X), 1)
        even_head = lane < GMLP_HEAD_DIM
        cs = slice(j * LANES_V7X, (j + 1) * LANES_V7X)
        vb = vbuf_ref[rows, cs]
        zero = jnp.zeros_like(vb)
        rhs = jnp.concatenate(
            [jnp.where(even_head, vb, zero), jnp.where(even_head, zero, vb)], axis=0)
        sp = _dot(wsp_ref[j], rhs) + bsp_ref[:, cs]
        u = _gelu(z_ref[r % 2, zrows, 2 * cw + j * LANES_V7X:2 * cw + (j + 1) * LANES_V7X])
        mix_ref[rows, cw + j * LANES_V7X:cw + (j + 1) * LANES_V7X] = (u * sp).astype(BF16)

    def m_outdot(r, n0):
        rows = slice(r * sub, (r + 1) * sub)
        ns = slice(n0, n0 + MXU_COLS_V7X)
        h_ref[0, rows, ns] = x_ref[0, rows, ns] + _dot(mix_ref[rows, :], wout_ref[:, ns])

    def rms_tasks(r):
        return [part(m_rms, r, c0) for c0 in range(0, sub, WIDE_ROWS)]

    def indot_tasks(r):
        return [part(m_indot, r, n0) for n0 in range(0, 4 * cw, MXU_COLS_V7X)]

    def outdot_tasks(r):
        return [part(m_outdot, r, n0) for n0 in range(0, d, MXU_COLS_V7X)]

    def vector_tasks(r):
        tasks = [part(m_glu, r, j) for j in range(ngrp)]
        for c in range(sub // CONV_ROWS):
            tasks += [part(m_conv, r, c, j) for j in range(ngrp)]
            tasks += [part(m_ln, r, c * CONV_ROWS + c1) for c1 in range(0, CONV_ROWS, WIDE_ROWS)]
        tasks += [part(m_v, r, c0) for c0 in range(0, sub, WIDE_ROWS)]
        tasks += [part(m_sp, r, blk, j) for blk in range(sub // GMLP_BLOCK) for j in range(ngrp)]
        return tasks

    first_in = indot_tasks(0)
    for task in rms_tasks(0) + first_in[:len(first_in) // 2]:
        task()
    for r in range(n_sub):
        matmuls = first_in[len(first_in) // 2:] if r == 0 else []
        if r + 1 < n_sub:
            for task in rms_tasks(r + 1):
                task()
            matmuls += indot_tasks(r + 1)
        if r > 0:
            matmuls += outdot_tasks(r - 1)
        _interleave(vector_tasks(r), matmuls)
    for task in outdot_tasks(n_sub - 1):
        task()
    abuf_ref[:, 0:CONV_HIST, :] = abuf_ref[:, ts:ts + CONV_HIST, :]


def _ffn_kernel(h_ref, ng_ref, wu_ref, cw_ref, cb_ref, wd_ref, fg_ref, o_ref,
                ubuf_ref, carry_ref, hid_ref, *, ts, fc, dff):
    t = pl.program_id(1)
    n_chunks = dff // fc
    fgrp = fc // LANES_V7X
    pad = SUBLANES_V7X

    @pl.when(t == 0)
    def _():
        carry_ref[...] = jnp.zeros(carry_ref.shape, F32)

    h = h_ref[0]
    y = _rms(h, ng_ref[...]).astype(BF16)

    def conv3(u, c0, slot):
        outs = []
        for i in range(fgrp):
            gi = c0 // LANES_V7X + i
            ls = slice(c0 + i * LANES_V7X, c0 + (i + 1) * LANES_V7X)
            ui = u[:, i * LANES_V7X:(i + 1) * LANES_V7X]
            ubuf_ref[slot, i, 0:pad, :] = carry_ref[gi]
            ubuf_ref[slot, i, pad:pad + ts, :] = ui
            outs.append(cw_ref[0:1, ls] * ubuf_ref[slot, i, pad - 2:pad - 2 + ts, :]
                        + cw_ref[1:2, ls] * ubuf_ref[slot, i, pad - 1:pad - 1 + ts, :]
                        + cw_ref[2:3, ls] * ui + cb_ref[:, ls])
            carry_ref[gi] = ui[ts - pad:ts]
        return jnp.concatenate(outs, axis=1)

    for c in range(n_chunks):
        gs, vs = c * fc, dff + c * fc
        g = conv3(_dot(y, wu_ref[:, gs:gs + fc]), gs, (2 * c) % UBUF_SLOTS)
        v = conv3(_dot(y, wu_ref[:, vs:vs + fc]), vs, (2 * c + 1) % UBUF_SLOTS)
        hid_ref[:, gs:gs + fc] = (g * _sigmoid(g) * v).astype(BF16)
    hid = hid_ref[...]
    h2 = jnp.concatenate(
        [h[:, n0:n0 + MXU_COLS_V7X] + _dot(hid, wd_ref[:, n0:n0 + MXU_COLS_V7X])
         for n0 in range(0, h.shape[1], MXU_COLS_V7X)], axis=1)
    o_ref[0] = _rms(h2, fg_ref[...])


def _bf16_weight(w):
    w = w.astype(BF16)
    if (w.shape[1] // LANES_V7X) % SUBLANES_V7X == 0:
        w = jnp.pad(w, ((0, 0), (0, LANES_V7X)))
    return w


def _const_spec(shape):
    nd = len(shape)
    return pl.BlockSpec(shape, lambda b, t: (0,) * nd, pipeline_mode=pl.Buffered(1))


def kernel(x, mix_norm_g, w_in, b_in, conv_a_w, conv_a_b, ln_a_g, ln_a_b, ln_b_g, ln_b_b, w_spatial, b_spatial, w_out, ffn_norm_g, w_up, conv_f_w, conv_f_b, w_down, final_norm_g):
    bsz, seq, d = x.shape
    assert w_in.shape[0] == 1
    in_w = w_in.shape[2]
    cw = in_w // 4
    k_taps = conv_a_w.shape[1]
    heads = w_spatial.shape[1]
    dff = w_down.shape[1]
    assert cw == heads * GMLP_HEAD_DIM and w_spatial.shape[2] == GMLP_BLOCK
    assert k_taps - 1 <= CONV_HIST
    assert seq % TS_MIX == 0 and seq % TS_FFN == 0 and TS_MIX % SUB_MIX == 0
    assert SUB_MIX % CONV_ROWS == 0 and CONV_ROWS % WIDE_ROWS == 0 and SUB_MIX % NARROW_ROWS == 0
    assert dff % FFN_CHUNK == 0 and conv_f_w.shape[1] == FFN_TAPS

    row = lambda v: v.reshape(1, -1).astype(F32)

    w_in_b = _bf16_weight(w_in[0])
    w_out_b = _bf16_weight(w_out[0])
    ws_cat = (w_spatial[0].reshape(heads // 2, 2, GMLP_BLOCK, GMLP_BLOCK)
              .transpose(0, 2, 1, 3).reshape(heads // 2, GMLP_BLOCK, 2 * GMLP_BLOCK))
    bsp = jnp.repeat(jnp.transpose(b_spatial[0]), GMLP_HEAD_DIM, axis=1)

    cparams = pltpu.CompilerParams(
        dimension_semantics=("arbitrary", "arbitrary"),
        vmem_limit_bytes=VMEM_LIMIT_BYTES_V7X)

    ts = TS_MIX
    h1 = pl.pallas_call(
        functools.partial(_mixer_kernel, ts=ts, cw=cw, k_taps=k_taps),
        out_shape=jax.ShapeDtypeStruct((bsz, seq, d), F32),
        grid=(bsz, seq // ts),
        in_specs=[
            pl.BlockSpec((1, ts, d), lambda b, t: (b, t, 0)),
            _const_spec((1, d)),
            _const_spec(w_in_b.shape),
            _const_spec((1, in_w)),
            _const_spec((k_taps, cw)),
            _const_spec((1, cw)),
            _const_spec((1, cw)),
            _const_spec((1, cw)),
            _const_spec((1, cw)),
            _const_spec((1, cw)),
            _const_spec((heads // 2, GMLP_BLOCK, 2 * GMLP_BLOCK)),
            _const_spec((GMLP_BLOCK, cw)),
            _const_spec(w_out_b.shape),
        ],
        out_specs=pl.BlockSpec((1, ts, d), lambda b, t: (b, t, 0)),
        scratch_shapes=[
            pltpu.VMEM((heads // 2, GMLP_BLOCK, 2 * GMLP_BLOCK), BF16),
            pltpu.VMEM((2, SUB_MIX, d), BF16),
            pltpu.VMEM((2, SUB_MIX, in_w), F32),
            pltpu.VMEM((cw // LANES_V7X, CONV_HIST + ts, LANES_V7X), F32),
            pltpu.VMEM((ts, cw), F32),
            pltpu.VMEM((ts, cw), BF16),
            pltpu.VMEM((ts, 2 * cw), BF16),
        ],
        compiler_params=cparams,
        name="mixer",
    )(x, row(mix_norm_g[0]), w_in_b, row(b_in[0]), conv_a_w[0], row(conv_a_b[0]),
      row(ln_a_g[0]), row(ln_a_b[0]), row(ln_b_g[0]), row(ln_b_b[0]), ws_cat, bsp, w_out_b)

    ts = TS_FFN
    fc = FFN_CHUNK
    w_up_b = _bf16_weight(w_up[0])
    w_down_b = w_down[0].astype(BF16)
    out = pl.pallas_call(
        functools.partial(_ffn_kernel, ts=ts, fc=fc, dff=dff),
        out_shape=jax.ShapeDtypeStruct((bsz, seq, d), F32),
        grid=(bsz, seq // ts),
        in_specs=[
            pl.BlockSpec((1, ts, d), lambda b, t: (b, t, 0)),
            _const_spec((1, d)),
            _const_spec(w_up_b.shape),
            _const_spec((FFN_TAPS, 2 * dff)),
            _const_spec((1, 2 * dff)),
            _const_spec(w_down_b.shape),
            _const_spec((1, d)),
        ],
        out_specs=pl.BlockSpec((1, ts, d), lambda b, t: (b, t, 0)),
        scratch_shapes=[
            pltpu.VMEM((UBUF_SLOTS, fc // LANES_V7X, SUBLANES_V7X + ts, LANES_V7X), F32),
            pltpu.VMEM((2 * dff // LANES_V7X, SUBLANES_V7X, LANES_V7X), F32),
            pltpu.VMEM((ts, dff), BF16),
        ],
        compiler_params=cparams,
        name="ffn",
    )(h1, row(ffn_norm_g[0]), w_up_b, conv_f_w[0], row(conv_f_b[0]), w_down_b,
      row(final_norm_g))
    return out
```

```python
import functools
import math

import jax
import jax.numpy as jnp
from jax import lax
from jax.experimental import pallas as pl
from jax.experimental.pallas import tpu as pltpu

F32 = jnp.float32
BF16 = jnp.bfloat16

NORM_EPS = 1e-6
LN_EPS = 1e-5

LANES_V7X = 128
SUBLANES_V7X = 8
BF16_ROWS_V7X = 16
MXU_COLS_V7X = 256
VMEM_LIMIT_BYTES_V7X = 56 * 1024 * 1024

GMLP_BLOCK = 128
GMLP_HEAD_DIM = 64
CONV_HIST = 32
FFN_TAPS = 3

TS_MIX = 1024
SUB_MIX = 2 * GMLP_BLOCK
TS_FFN = 1024
FFN_CHUNK = 256
UBUF_SLOTS = 4
CONV_ROWS = 64
WIDE_ROWS = BF16_ROWS_V7X
NARROW_ROWS = 2 * BF16_ROWS_V7X


def _rms(x, g):
    ms = jnp.mean(x * x, axis=-1, keepdims=True)
    return (x * lax.rsqrt(ms + NORM_EPS)) * g


def _layer_norm(x, g, b):
    mu = jnp.mean(x, axis=-1, keepdims=True)
    xc = x - mu
    var = jnp.mean(xc * xc, axis=-1, keepdims=True)
    return (xc * lax.rsqrt(var + LN_EPS)) * g + b


def _gelu(x):
    return 0.5 * x * (1.0 + lax.erf(x * (1.0 / math.sqrt(2.0))))


def _sigmoid(x):
    return 1.0 / (1.0 + jnp.exp2(x * (-1.0 / math.log(2.0))))


def _dot(a, b):
    return jnp.dot(a, b, preferred_element_type=F32)


def _interleave(major, minor):
    done = 0
    for i, task in enumerate(major):
        task()
        want = (i + 1) * len(minor) // len(major)
        for t in minor[done:want]:
            t()
        done = want


def _mixer_kernel(x_ref, ng_ref, win_ref, bin_ref, cw_ref, cb_ref, lag_ref,
                  lab_ref, lbg_ref, lbb_ref, ws_ref, bsp_ref, wout_ref,
                  h_ref, wsp_ref, ysub_ref, z_ref, abuf_ref, cbuf_ref, vbuf_ref, mix_ref,
                  *, ts, cw, k_taps):
    b = pl.program_id(0)
    t = pl.program_id(1)
    ngrp = cw // LANES_V7X
    sub = SUB_MIX
    n_sub = ts // sub
    d = h_ref.shape[2]
    part = functools.partial

    @pl.when(jnp.logical_and(b == 0, t == 0))
    def _():
        row = lax.broadcasted_iota(jnp.int32, (GMLP_BLOCK, 2 * GMLP_BLOCK), 0)
        col = lax.broadcasted_iota(jnp.int32, (GMLP_BLOCK, 2 * GMLP_BLOCK), 1)
        tril = (col % GMLP_BLOCK) <= row
        for j in range(ngrp):
            wsp_ref[j] = jnp.where(tril, ws_ref[j], 0.0).astype(BF16)

    @pl.when(t == 0)
    def _():
        abuf_ref[:, 0:CONV_HIST, :] = jnp.zeros((ngrp, CONV_HIST, LANES_V7X), F32)

    def m_rms(r, c0):
        xs = x_ref[0, r * sub + c0:r * sub + c0 + WIDE_ROWS, :]
        ysub_ref[r % 2, c0:c0 + WIDE_ROWS, :] = _rms(xs, ng_ref[...]).astype(BF16)

    def m_indot(r, n0):
        ns = slice(n0, n0 + MXU_COLS_V7X)
        z_ref[r % 2, :, ns] = _dot(ysub_ref[r % 2], win_ref[:, ns]) + bin_ref[:, ns]

    def m_glu(r, j):
        for c0 in range(0, sub, NARROW_ROWS):
            rs = slice(c0, c0 + NARROW_ROWS)
            a0 = CONV_HIST + r * sub + c0
            abuf_ref[j, a0:a0 + NARROW_ROWS, :] = (
                z_ref[r % 2, rs, j * LANES_V7X:(j + 1) * LANES_V7X]
                * _sigmoid(z_ref[r % 2, rs, cw + j * LANES_V7X:cw + (j + 1) * LANES_V7X]))

    def m_v(r, c0):
        v = _layer_norm(_gelu(z_ref[r % 2, c0:c0 + WIDE_ROWS, 3 * cw:4 * cw]),
                        lbg_ref[...], lbb_ref[...])
        vbuf_ref[r * sub + c0:r * sub + c0 + WIDE_ROWS, :] = v.astype(BF16)

    def m_conv(r, c, j):
        c0 = r * sub + c * CONV_ROWS
        first = CONV_HIST - (k_taps - 1)
        cs = slice(j * LANES_V7X, (j + 1) * LANES_V7X)
        acc = jnp.broadcast_to(cb_ref[:, cs], (CONV_ROWS, LANES_V7X))
        for k in range(k_taps):
            lo = c0 + first + k
            acc = acc + cw_ref[k:k + 1, cs] * abuf_ref[j, lo:lo + CONV_ROWS, :]
        cbuf_ref[c0:c0 + CONV_ROWS, cs] = acc

    def m_ln(r, c1):
        rs = slice(r * sub + c1, r * sub + c1 + WIDE_ROWS)
        a = _layer_norm(cbuf_ref[rs, :], lag_ref[...], lab_ref[...])
        mix_ref[rs, 0:cw] = (a * _sigmoid(a)).astype(BF16)

    def m_sp(r, blk, j):
        zrows = slice(blk * GMLP_BLOCK, (blk + 1) * GMLP_BLOCK)
        rows = slice(r * sub + blk * GMLP_BLOCK, r * sub + (blk + 1) * GMLP_BLOCK)
        lane = lax.broadcasted_iota(jnp.int32, (GMLP_BLOCK, LANES_V7X), 1)
        even_head = lane < GMLP_HEAD_DIM
        cs = slice(j * LANES_V7X, (j + 1) * LANES_V7X)
        vb = vbuf_ref[rows, cs]
        zero = jnp.zeros_like(vb)
        rhs = jnp.concatenate(
            [jnp.where(even_head, vb, zero), jnp.where(even_head, zero, vb)], axis=0)
        sp = _dot(wsp_ref[j], rhs) + bsp_ref[:, cs]
        u = _gelu(z_ref[r % 2, zrows, 2 * cw + j * LANES_V7X:2 * cw + (j + 1) * LANES_V7X])
        mix_ref[rows, cw + j * LANES_V7X:cw + (j + 1) * LANES_V7X] = (u * sp).astype(BF16)

    def m_outdot(r, n0):
        rows = slice(r * sub, (r + 1) * sub)
        ns = slice(n0, n0 + MXU_COLS_V7X)
        h_ref[0, rows, ns] = x_ref[0, rows, ns] + _dot(mix_ref[rows, :], wout_ref[:, ns])

    def rms_tasks(r):
        return [part(m_rms, r, c0) for c0 in range(0, sub, WIDE_ROWS)]

    def indot_tasks(r):
        return [part(m_indot, r, n0) for n0 in range(0, 4 * cw, MXU_COLS_V7X)]

    def outdot_tasks(r):
        return [part(m_outdot, r, n0) for n0 in range(0, d, MXU_COLS_V7X)]

    def vector_tasks(r):
        tasks = [part(m_glu, r, j) for j in range(ngrp)]
        for c in range(sub // CONV_ROWS):
            tasks += [part(m_conv, r, c, j) for j in range(ngrp)]
            tasks += [part(m_ln, r, c * CONV_ROWS + c1) for c1 in range(0, CONV_ROWS, WIDE_ROWS)]
        tasks += [part(m_v, r, c0) for c0 in range(0, sub, WIDE_ROWS)]
        tasks += [part(m_sp, r, blk, j) for blk in range(sub // GMLP_BLOCK) for j in range(ngrp)]
        return tasks

    first_in = indot_tasks(0)
    for task in rms_tasks(0) + first_in[:len(first_in) // 2]:
        task()
    for r in range(n_sub):
        matmuls = first_in[len(first_in) // 2:] if r == 0 else []
        if r + 1 < n_sub:
            for task in rms_tasks(r + 1):
                task()
            matmuls += indot_tasks(r + 1)
        if r > 0:
            matmuls += outdot_tasks(r - 1)
        _interleave(vector_tasks(r), matmuls)
    for task in outdot_tasks(n_sub - 1):
        task()
    abuf_ref[:, 0:CONV_HIST, :] = abuf_ref[:, ts:ts + CONV_HIST, :]


def _ffn_kernel(h_ref, ng_ref, wu_ref, cw_ref, cb_ref, wd_ref, fg_ref, o_ref,
                ubuf_ref, carry_ref, hid_ref, *, ts, fc, dff):
    t = pl.program_id(1)
    n_chunks = dff // fc
    fgrp = fc // LANES_V7X
    pad = SUBLANES_V7X

    @pl.when(t == 0)
    def _():
        carry_ref[...] = jnp.zeros(carry_ref.shape, F32)

    h = h_ref[0]
    y = _rms(h, ng_ref[...]).astype(BF16)

    def conv3(u, c0, slot):
        outs = []
        for i in range(fgrp):
            gi = c0 // LANES_V7X + i
            ls = slice(c0 + i * LANES_V7X, c0 + (i + 1) * LANES_V7X)
            ui = u[:, i * LANES_V7X:(i + 1) * LANES_V7X]
            ubuf_ref[slot, i, 0:pad, :] = carry_ref[gi]
            ubuf_ref[slot, i, pad:pad + ts, :] = ui
            outs.append(cw_ref[0:1, ls] * ubuf_ref[slot, i, pad - 2:pad - 2 + ts, :]
                        + cw_ref[1:2, ls] * ubuf_ref[slot, i, pad - 1:pad - 1 + ts, :]
                        + cw_ref[2:3, ls] * ui + cb_ref[:, ls])
            carry_ref[gi] = ui[ts - pad:ts]
        return jnp.concatenate(outs, axis=1)

    for c in range(n_chunks):
        gs, vs = c * fc, dff + c * fc
        g = conv3(_dot(y, wu_ref[:, gs:gs + fc]), gs, (2 * c) % UBUF_SLOTS)
        v = conv3(_dot(y, wu_ref[:, vs:vs + fc]), vs, (2 * c + 1) % UBUF_SLOTS)
        hid_ref[:, gs:gs + fc] = (g * _sigmoid(g) * v).astype(BF16)
    for r0 in range(0, ts, ts // 2):
        rows = slice(r0, r0 + ts // 2)
        hid = hid_ref[rows, :]
        h2 = jnp.concatenate(
            [h[rows, n0:n0 + MXU_COLS_V7X] + _dot(hid, wd_ref[:, n0:n0 + MXU_COLS_V7X])
             for n0 in range(0, h.shape[1], MXU_COLS_V7X)], axis=1)
        o_ref[0, rows, :] = _rms(h2, fg_ref[...])


def _bf16_weight(w):
    w = w.astype(BF16)
    if (w.shape[1] // LANES_V7X) % SUBLANES_V7X == 0:
        w = jnp.pad(w, ((0, 0), (0, LANES_V7X)))
    return w


def _const_spec(shape):
    nd = len(shape)
    return pl.BlockSpec(shape, lambda b, t: (0,) * nd, pipeline_mode=pl.Buffered(1))


def kernel(x, mix_norm_g, w_in, b_in, conv_a_w, conv_a_b, ln_a_g, ln_a_b, ln_b_g, ln_b_b, w_spatial, b_spatial, w_out, ffn_norm_g, w_up, conv_f_w, conv_f_b, w_down, final_norm_g):
    bsz, seq, d = x.shape
    assert w_in.shape[0] == 1
    in_w = w_in.shape[2]
    cw = in_w // 4
    k_taps = conv_a_w.shape[1]
    heads = w_spatial.shape[1]
    dff = w_down.shape[1]
    assert cw == heads * GMLP_HEAD_DIM and w_spatial.shape[2] == GMLP_BLOCK
    assert k_taps - 1 <= CONV_HIST
    assert seq % TS_MIX == 0 and seq % TS_FFN == 0 and TS_MIX % SUB_MIX == 0
    assert SUB_MIX % CONV_ROWS == 0 and CONV_ROWS % WIDE_ROWS == 0 and SUB_MIX % NARROW_ROWS == 0
    assert dff % FFN_CHUNK == 0 and conv_f_w.shape[1] == FFN_TAPS

    row = lambda v: v.reshape(1, -1).astype(F32)

    w_in_b = _bf16_weight(w_in[0])
    w_out_b = _bf16_weight(w_out[0])
    ws_cat = (w_spatial[0].reshape(heads // 2, 2, GMLP_BLOCK, GMLP_BLOCK)
              .transpose(0, 2, 1, 3).reshape(heads // 2, GMLP_BLOCK, 2 * GMLP_BLOCK))
    bsp = jnp.repeat(jnp.transpose(b_spatial[0]), GMLP_HEAD_DIM, axis=1)

    cparams = pltpu.CompilerParams(
        dimension_semantics=("arbitrary", "arbitrary"),
        vmem_limit_bytes=VMEM_LIMIT_BYTES_V7X)

    ts = TS_MIX
    h1 = pl.pallas_call(
        functools.partial(_mixer_kernel, ts=ts, cw=cw, k_taps=k_taps),
        out_shape=jax.ShapeDtypeStruct((bsz, seq, d), F32),
        grid=(bsz, seq // ts),
        in_specs=[
            pl.BlockSpec((1, ts, d), lambda b, t: (b, t, 0)),
            _const_spec((1, d)),
            _const_spec(w_in_b.shape),
            _const_spec((1, in_w)),
            _const_spec((k_taps, cw)),
            _const_spec((1, cw)),
            _const_spec((1, cw)),
            _const_spec((1, cw)),
            _const_spec((1, cw)),
            _const_spec((1, cw)),
            _const_spec((heads // 2, GMLP_BLOCK, 2 * GMLP_BLOCK)),
            _const_spec((GMLP_BLOCK, cw)),
            _const_spec(w_out_b.shape),
        ],
        out_specs=pl.BlockSpec((1, ts, d), lambda b, t: (b, t, 0)),
        scratch_shapes=[
            pltpu.VMEM((heads // 2, GMLP_BLOCK, 2 * GMLP_BLOCK), BF16),
            pltpu.VMEM((2, SUB_MIX, d), BF16),
            pltpu.VMEM((2, SUB_MIX, in_w), F32),
            pltpu.VMEM((cw // LANES_V7X, CONV_HIST + ts, LANES_V7X), F32),
            pltpu.VMEM((ts, cw), F32),
            pltpu.VMEM((ts, cw), BF16),
            pltpu.VMEM((ts, 2 * cw), BF16),
        ],
        compiler_params=cparams,
        name="mixer",
    )(x, row(mix_norm_g[0]), w_in_b, row(b_in[0]), conv_a_w[0], row(conv_a_b[0]),
      row(ln_a_g[0]), row(ln_a_b[0]), row(ln_b_g[0]), row(ln_b_b[0]), ws_cat, bsp, w_out_b)

    ts = TS_FFN
    fc = FFN_CHUNK
    w_up_b = _bf16_weight(w_up[0])
    w_down_b = w_down[0].astype(BF16)
    out = pl.pallas_call(
        functools.partial(_ffn_kernel, ts=ts, fc=fc, dff=dff),
        out_shape=jax.ShapeDtypeStruct((bsz, seq, d), F32),
        grid=(bsz, seq // ts),
        in_specs=[
            pl.BlockSpec((1, ts, d), lambda b, t: (b, t, 0)),
            _const_spec((1, d)),
            _const_spec(w_up_b.shape),
            _const_spec((FFN_TAPS, 2 * dff)),
            _const_spec((1, 2 * dff)),
            _const_spec(w_down_b.shape),
            _const_spec((1, d)),
        ],
        out_specs=pl.BlockSpec((1, ts, d), lambda b, t: (b, t, 0)),
        scratch_shapes=[
            pltpu.VMEM((UBUF_SLOTS, fc // LANES_V7X, SUBLANES_V7X + ts, LANES_V7X), F32),
            pltpu.VMEM((2 * dff // LANES_V7X, SUBLANES_V7X, LANES_V7X), F32),
            pltpu.VMEM((ts, dff), BF16),
        ],
        compiler_params=cparams,
        name="ffn",
    )(h1, row(ffn_norm_g[0]), w_up_b, conv_f_w[0], row(conv_f_b[0]), w_down_b,
      row(final_norm_g))
    return out
```

```python
import functools
import math

import jax
import jax.numpy as jnp
from jax import lax
from jax.experimental import pallas as pl
from jax.experimental.pallas import tpu as pltpu

F32 = jnp.float32
BF16 = jnp.bfloat16

NORM_EPS = 1e-6
LN_EPS = 1e-5

LANES_V7X = 128
SUBLANES_V7X = 8
BF16_ROWS_V7X = 16
MXU_COLS_V7X = 256
VMEM_LIMIT_BYTES_V7X = 56 * 1024 * 1024

GMLP_BLOCK = 128
GMLP_HEAD_DIM = 64
CONV_HIST = 32
FFN_TAPS = 3

TS_MIX = 1024
SUB_MIX = 2 * GMLP_BLOCK
TS_FFN = 1024
FFN_CHUNK = 256
UBUF_SLOTS = 4
CONV_ROWS = 64
WIDE_ROWS = 2 * BF16_ROWS_V7X
NARROW_ROWS = 2 * BF16_ROWS_V7X


def _rms(x, g):
    ms = jnp.mean(x * x, axis=-1, keepdims=True)
    return (x * lax.rsqrt(ms + NORM_EPS)) * g


def _layer_norm(x, g, b):
    mu = jnp.mean(x, axis=-1, keepdims=True)
    xc = x - mu
    var = jnp.mean(xc * xc, axis=-1, keepdims=True)
    return (xc * lax.rsqrt(var + LN_EPS)) * g + b


def _gelu(x):
    return 0.5 * x * (1.0 + lax.erf(x * (1.0 / math.sqrt(2.0))))


def _sigmoid(x):
    return 1.0 / (1.0 + jnp.exp2(x * (-1.0 / math.log(2.0))))


def _dot(a, b):
    return jnp.dot(a, b, preferred_element_type=F32)


def _interleave(major, minor):
    done = 0
    for i, task in enumerate(major):
        task()
        want = (i + 1) * len(minor) // len(major)
        for t in minor[done:want]:
            t()
        done = want


def _mixer_kernel(x_ref, ng_ref, win_ref, bin_ref, cw_ref, cb_ref, lag_ref,
                  lab_ref, lbg_ref, lbb_ref, ws_ref, bsp_ref, wout_ref,
                  h_ref, wsp_ref, ysub_ref, z_ref, abuf_ref, cbuf_ref, vbuf_ref, mix_ref,
                  *, ts, cw, k_taps):
    b = pl.program_id(0)
    t = pl.program_id(1)
    ngrp = cw // LANES_V7X
    sub = SUB_MIX
    n_sub = ts // sub
    d = h_ref.shape[2]
    part = functools.partial

    @pl.when(jnp.logical_and(b == 0, t == 0))
    def _():
        row = lax.broadcasted_iota(jnp.int32, (GMLP_BLOCK, 2 * GMLP_BLOCK), 0)
        col = lax.broadcasted_iota(jnp.int32, (GMLP_BLOCK, 2 * GMLP_BLOCK), 1)
        tril = (col % GMLP_BLOCK) <= row
        for j in range(ngrp):
            wsp_ref[j] = jnp.where(tril, ws_ref[j], 0.0).astype(BF16)

    @pl.when(t == 0)
    def _():
        abuf_ref[:, 0:CONV_HIST, :] = jnp.zeros((ngrp, CONV_HIST, LANES_V7X), F32)

    def m_rms(r, c0):
        xs = x_ref[0, r * sub + c0:r * sub + c0 + WIDE_ROWS, :]
        ysub_ref[r % 2, c0:c0 + WIDE_ROWS, :] = _rms(xs, ng_ref[...]).astype(BF16)

    def m_indot(r, n0):
        ns = slice(n0, n0 + MXU_COLS_V7X)
        z_ref[r % 2, :, ns] = _dot(ysub_ref[r % 2], win_ref[:, ns]) + bin_ref[:, ns]

    def m_glu(r, j):
        for c0 in range(0, sub, NARROW_ROWS):
            rs = slice(c0, c0 + NARROW_ROWS)
            a0 = CONV_HIST + r * sub + c0
            abuf_ref[j, a0:a0 + NARROW_ROWS, :] = (
                z_ref[r % 2, rs, j * LANES_V7X:(j + 1) * LANES_V7X]
                * _sigmoid(z_ref[r % 2, rs, cw + j * LANES_V7X:cw + (j + 1) * LANES_V7X]))

    def m_v(r, c0):
        v = _layer_norm(_gelu(z_ref[r % 2, c0:c0 + WIDE_ROWS, 3 * cw:4 * cw]),
                        lbg_ref[...], lbb_ref[...])
        vbuf_ref[r * sub + c0:r * sub + c0 + WIDE_ROWS, :] = v.astype(BF16)

    def m_conv(r, c, j):
        c0 = r * sub + c * CONV_ROWS
        first = CONV_HIST - (k_taps - 1)
        cs = slice(j * LANES_V7X, (j + 1) * LANES_V7X)
        acc = jnp.broadcast_to(cb_ref[:, cs], (CONV_ROWS, LANES_V7X))
        for k in range(k_taps):
            lo = c0 + first + k
            acc = acc + cw_ref[k:k + 1, cs] * abuf_ref[j, lo:lo + CONV_ROWS, :]
        cbuf_ref[c0:c0 + CONV_ROWS, cs] = acc

    def m_ln(r, c1):
        rs = slice(r * sub + c1, r * sub + c1 + WIDE_ROWS)
        a = _layer_norm(cbuf_ref[rs, :], lag_ref[...], lab_ref[...])
        mix_ref[rs, 0:cw] = (a * _sigmoid(a)).astype(BF16)

    def m_sp(r, blk, j):
        zrows = slice(blk * GMLP_BLOCK, (blk + 1) * GMLP_BLOCK)
        rows = slice(r * sub + blk * GMLP_BLOCK, r * sub + (blk + 1) * GMLP_BLOCK)
        lane = lax.broadcasted_iota(jnp.int32, (GMLP_BLOCK, LANES_V7X), 1)
        even_head = lane < GMLP_HEAD_DIM
        cs = slice(j * LANES_V7X, (j + 1) * LANES_V7X)
        vb = vbuf_ref[rows, cs]
        zero = jnp.zeros_like(vb)
        rhs = jnp.concatenate(
            [jnp.where(even_head, vb, zero), jnp.where(even_head, zero, vb)], axis=0)
        sp = _dot(wsp_ref[j], rhs) + bsp_ref[:, cs]
        u = _gelu(z_ref[r % 2, zrows, 2 * cw + j * LANES_V7X:2 * cw + (j + 1) * LANES_V7X])
        mix_ref[rows, cw + j * LANES_V7X:cw + (j + 1) * LANES_V7X] = (u * sp).astype(BF16)

    def m_outdot(r, n0):
        rows = slice(r * sub, (r + 1) * sub)
        ns = slice(n0, n0 + MXU_COLS_V7X)
        h_ref[0, rows, ns] = x_ref[0, rows, ns] + _dot(mix_ref[rows, :], wout_ref[:, ns])

    def rms_tasks(r):
        return [part(m_rms, r, c0) for c0 in range(0, sub, WIDE_ROWS)]

    def indot_tasks(r):
        return [part(m_indot, r, n0) for n0 in range(0, 4 * cw, MXU_COLS_V7X)]

    def outdot_tasks(r):
        return [part(m_outdot, r, n0) for n0 in range(0, d, MXU_COLS_V7X)]

    def vector_tasks(r):
        tasks = [part(m_glu, r, j) for j in range(ngrp)]
        for c in range(sub // CONV_ROWS):
            tasks += [part(m_conv, r, c, j) for j in range(ngrp)]
            tasks += [part(m_ln, r, c * CONV_ROWS + c1) for c1 in range(0, CONV_ROWS, WIDE_ROWS)]
        tasks += [part(m_v, r, c0) for c0 in range(0, sub, WIDE_ROWS)]
        tasks += [part(m_sp, r, blk, j) for blk in range(sub // GMLP_BLOCK) for j in range(ngrp)]
        return tasks

    first_in = indot_tasks(0)
    for task in rms_tasks(0) + first_in[:len(first_in) // 2]:
        task()
    for r in range(n_sub):
        matmuls = first_in[len(first_in) // 2:] if r == 0 else []
        if r + 1 < n_sub:
            for task in rms_tasks(r + 1):
                task()
            matmuls += indot_tasks(r + 1)
        if r > 0:
            matmuls += outdot_tasks(r - 1)
        _interleave(vector_tasks(r), matmuls)
    for task in outdot_tasks(n_sub - 1):
        task()
    abuf_ref[:, 0:CONV_HIST, :] = abuf_ref[:, ts:ts + CONV_HIST, :]


def _ffn_kernel(h_ref, ng_ref, wu_ref, cw_ref, cb_ref, wd_ref, fg_ref, o_ref,
                ubuf_ref, carry_ref, hid_ref, *, ts, fc, dff):
    t = pl.program_id(1)
    n_chunks = dff // fc
    fgrp = fc // LANES_V7X
    pad = SUBLANES_V7X

    @pl.when(t == 0)
    def _():
        carry_ref[...] = jnp.zeros(carry_ref.shape, F32)

    h = h_ref[0]
    y = _rms(h, ng_ref[...]).astype(BF16)

    def conv3(u, c0, slot):
        outs = []
        for i in range(fgrp):
            gi = c0 // LANES_V7X + i
            ls = slice(c0 + i * LANES_V7X, c0 + (i + 1) * LANES_V7X)
            ui = u[:, i * LANES_V7X:(i + 1) * LANES_V7X]
            ubuf_ref[slot, i, 0:pad, :] = carry_ref[gi]
            ubuf_ref[slot, i, pad:pad + ts, :] = ui
            outs.append(cw_ref[0:1, ls] * ubuf_ref[slot, i, pad - 2:pad - 2 + ts, :]
                        + cw_ref[1:2, ls] * ubuf_ref[slot, i, pad - 1:pad - 1 + ts, :]
                        + cw_ref[2:3, ls] * ui + cb_ref[:, ls])
            carry_ref[gi] = ui[ts - pad:ts]
        return jnp.concatenate(outs, axis=1)

    for c in range(n_chunks):
        gs, vs = c * fc, dff + c * fc
        g = conv3(_dot(y, wu_ref[:, gs:gs + fc]), gs, (2 * c) % UBUF_SLOTS)
        v = conv3(_dot(y, wu_ref[:, vs:vs + fc]), vs, (2 * c + 1) % UBUF_SLOTS)
        hid_ref[:, gs:gs + fc] = (g * _sigmoid(g) * v).astype(BF16)
    hid = hid_ref[...]
    h2 = jnp.concatenate(
        [h[:, n0:n0 + MXU_COLS_V7X] + _dot(hid, wd_ref[:, n0:n0 + MXU_COLS_V7X])
         for n0 in range(0, h.shape[1], MXU_COLS_V7X)], axis=1)
    o_ref[0] = _rms(h2, fg_ref[...])


def _bf16_weight(w):
    w = w.astype(BF16)
    if (w.shape[1] // LANES_V7X) % SUBLANES_V7X == 0:
        w = jnp.pad(w, ((0, 0), (0, LANES_V7X)))
    return w


def _const_spec(shape):
    nd = len(shape)
    return pl.BlockSpec(shape, lambda b, t: (0,) * nd, pipeline_mode=pl.Buffered(1))


def kernel(x, mix_norm_g, w_in, b_in, conv_a_w, conv_a_b, ln_a_g, ln_a_b, ln_b_g, ln_b_b, w_spatial, b_spatial, w_out, ffn_norm_g, w_up, conv_f_w, conv_f_b, w_down, final_norm_g):
    bsz, seq, d = x.shape
    assert w_in.shape[0] == 1
    in_w = w_in.shape[2]
    cw = in_w // 4
    k_taps = conv_a_w.shape[1]
    heads = w_spatial.shape[1]
    dff = w_down.shape[1]
    assert cw == heads * GMLP_HEAD_DIM and w_spatial.shape[2] == GMLP_BLOCK
    assert k_taps - 1 <= CONV_HIST
    assert seq % TS_MIX == 0 and seq % TS_FFN == 0 and TS_MIX % SUB_MIX == 0
    assert SUB_MIX % CONV_ROWS == 0 and CONV_ROWS % WIDE_ROWS == 0 and SUB_MIX % NARROW_ROWS == 0
    assert dff % FFN_CHUNK == 0 and conv_f_w.shape[1] == FFN_TAPS

    row = lambda v: v.reshape(1, -1).astype(F32)

    w_in_b = _bf16_weight(w_in[0])
    w_out_b = _bf16_weight(w_out[0])
    ws_cat = (w_spatial[0].reshape(heads // 2, 2, GMLP_BLOCK, GMLP_BLOCK)
              .transpose(0, 2, 1, 3).reshape(heads // 2, GMLP_BLOCK, 2 * GMLP_BLOCK))
    bsp = jnp.repeat(jnp.transpose(b_spatial[0]), GMLP_HEAD_DIM, axis=1)

    cparams = pltpu.CompilerParams(
        dimension_semantics=("arbitrary", "arbitrary"),
        vmem_limit_bytes=VMEM_LIMIT_BYTES_V7X)

    ts = TS_MIX
    h1 = pl.pallas_call(
        functools.partial(_mixer_kernel, ts=ts, cw=cw, k_taps=k_taps),
        out_shape=jax.ShapeDtypeStruct((bsz, seq, d), F32),
        grid=(bsz, seq // ts),
        in_specs=[
            pl.BlockSpec((1, ts, d), lambda b, t: (b, t, 0)),
            _const_spec((1, d)),
            _const_spec(w_in_b.shape),
            _const_spec((1, in_w)),
            _const_spec((k_taps, cw)),
            _const_spec((1, cw)),
            _const_spec((1, cw)),
            _const_spec((1, cw)),
            _const_spec((1, cw)),
            _const_spec((1, cw)),
            _const_spec((heads // 2, GMLP_BLOCK, 2 * GMLP_BLOCK)),
            _const_spec((GMLP_BLOCK, cw)),
            _const_spec(w_out_b.shape),
        ],
        out_specs=pl.BlockSpec((1, ts, d), lambda b, t: (b, t, 0)),
        scratch_shapes=[
            pltpu.VMEM((heads // 2, GMLP_BLOCK, 2 * GMLP_BLOCK), BF16),
            pltpu.VMEM((2, SUB_MIX, d), BF16),
            pltpu.VMEM((2, SUB_MIX, in_w), F32),
            pltpu.VMEM((cw // LANES_V7X, CONV_HIST + ts, LANES_V7X), F32),
            pltpu.VMEM((ts, cw), F32),
            pltpu.VMEM((ts, cw), BF16),
            pltpu.VMEM((ts, 2 * cw), BF16),
        ],
        compiler_params=cparams,
        name="mixer",
    )(x, row(mix_norm_g[0]), w_in_b, row(b_in[0]), conv_a_w[0], row(conv_a_b[0]),
      row(ln_a_g[0]), row(ln_a_b[0]), row(ln_b_g[0]), row(ln_b_b[0]), ws_cat, bsp, w_out_b)

    ts = TS_FFN
    fc = FFN_CHUNK
    w_up_b = _bf16_weight(w_up[0])
    w_down_b = w_down[0].astype(BF16)
    out = pl.pallas_call(
        functools.partial(_ffn_kernel, ts=ts, fc=fc, dff=dff),
        out_shape=jax.ShapeDtypeStruct((bsz, seq, d), F32),
        grid=(bsz, seq // ts),
        in_specs=[
            pl.BlockSpec((1, ts, d), lambda b, t: (b, t, 0)),
            _const_spec((1, d)),
            _const_spec(w_up_b.shape),
            _const_spec((FFN_TAPS, 2 * dff)),
            _const_spec((1, 2 * dff)),
            _const_spec(w_down_b.shape),
            _const_spec((1, d)),
        ],
        out_specs=pl.BlockSpec((1, ts, d), lambda b, t: (b, t, 0)),
        scratch_shapes=[
            pltpu.VMEM((UBUF_SLOTS, fc // LANES_V7X, SUBLANES_V7X + ts, LANES_V7X), F32),
            pltpu.VMEM((2 * dff // LANES_V7X, SUBLANES_V7X, LANES_V7X), F32),
            pltpu.VMEM((ts, dff), BF16),
        ],
        compiler_params=cparams,
        name="ffn",
    )(h1, row(ffn_norm_g[0]), w_up_b, conv_f_w[0], row(conv_f_b[0]), w_down_b,
      row(final_norm_g))
    return out
```

```python
import functools
import math

import jax
import jax.numpy as jnp
from jax import lax
from jax.experimental import pallas as pl
from jax.experimental.pallas import tpu as pltpu

F32 = jnp.float32
BF16 = jnp.bfloat16

NORM_EPS = 1e-6
LN_EPS = 1e-5

LANES_V7X = 128
SUBLANES_V7X = 8
BF16_ROWS_V7X = 16
MXU_COLS_V7X = 256
VMEM_LIMIT_BYTES_V7X = 56 * 1024 * 1024

GMLP_BLOCK = 128
GMLP_HEAD_DIM = 64
CONV_HIST = 32
FFN_TAPS = 3

TS_MIX = 1024
SUB_MIX = 2 * GMLP_BLOCK
TS_FFN = 1024
FFN_CHUNK = 256
UBUF_SLOTS = 4
CONV_ROWS = 64
WIDE_ROWS = 2 * BF16_ROWS_V7X
NARROW_ROWS = 2 * BF16_ROWS_V7X


def _rms(x, g):
    ms = jnp.mean(x * x, axis=-1, keepdims=True)
    return (x * lax.rsqrt(ms + NORM_EPS)) * g


def _layer_norm(x, g, b):
    mu = jnp.mean(x, axis=-1, keepdims=True)
    xc = x - mu
    var = jnp.mean(xc * xc, axis=-1, keepdims=True)
    return (xc * lax.rsqrt(var + LN_EPS)) * g + b


def _gelu(x):
    return 0.5 * x * (1.0 + lax.erf(x * (1.0 / math.sqrt(2.0))))


def _sigmoid(x):
    return 1.0 / (1.0 + jnp.exp2(x * (-1.0 / math.log(2.0))))


def _dot(a, b):
    return jnp.dot(a, b, preferred_element_type=F32)


def _interleave(major, minor):
    done = 0
    for i, task in enumerate(major):
        task()
        want = (i + 1) * len(minor) // len(major)
        for t in minor[done:want]:
            t()
        done = want


def _mixer_kernel(x_ref, ng_ref, win_ref, bin_ref, cw_ref, cb_ref, lag_ref,
                  lab_ref, lbg_ref, lbb_ref, ws_ref, bsp_ref, wout_ref,
                  h_ref, wsp_ref, ysub_ref, z_ref, abuf_ref, cbuf_ref, vbuf_ref, mix_ref,
                  *, ts, cw, k_taps):
    b = pl.program_id(0)
    t = pl.program_id(1)
    ngrp = cw // LANES_V7X
    sub = SUB_MIX
    n_sub = ts // sub
    d = h_ref.shape[2]
    part = functools.partial

    @pl.when(jnp.logical_and(b == 0, t == 0))
    def _():
        row = lax.broadcasted_iota(jnp.int32, (GMLP_BLOCK, 2 * GMLP_BLOCK), 0)
        col = lax.broadcasted_iota(jnp.int32, (GMLP_BLOCK, 2 * GMLP_BLOCK), 1)
        tril = (col % GMLP_BLOCK) <= row
        for j in range(ngrp):
            wsp_ref[j] = jnp.where(tril, ws_ref[j], 0.0).astype(BF16)

    @pl.when(t == 0)
    def _():
        abuf_ref[:, 0:CONV_HIST, :] = jnp.zeros((ngrp, CONV_HIST, LANES_V7X), F32)

    def m_rms(r, c0):
        xs = x_ref[0, r * sub + c0:r * sub + c0 + WIDE_ROWS, :]
        ysub_ref[r % 2, c0:c0 + WIDE_ROWS, :] = _rms(xs, ng_ref[...]).astype(BF16)

    def m_indot(r, n0):
        ns = slice(n0, n0 + MXU_COLS_V7X)
        z_ref[r % 2, :, ns] = _dot(ysub_ref[r % 2], win_ref[:, ns]) + bin_ref[:, ns]

    def m_glu(r, j):
        for c0 in range(0, sub, NARROW_ROWS):
            rs = slice(c0, c0 + NARROW_ROWS)
            a0 = CONV_HIST + r * sub + c0
            abuf_ref[j, a0:a0 + NARROW_ROWS, :] = (
                z_ref[r % 2, rs, j * LANES_V7X:(j + 1) * LANES_V7X]
                * _sigmoid(z_ref[r % 2, rs, cw + j * LANES_V7X:cw + (j + 1) * LANES_V7X]))

    def m_v(r, c0):
        v = _layer_norm(_gelu(z_ref[r % 2, c0:c0 + WIDE_ROWS, 3 * cw:4 * cw]),
                        lbg_ref[...], lbb_ref[...])
        vbuf_ref[r * sub + c0:r * sub + c0 + WIDE_ROWS, :] = v.astype(BF16)

    def m_conv(r, c, j):
        c0 = r * sub + c * CONV_ROWS
        first = CONV_HIST - (k_taps - 1)
        cs = slice(j * LANES_V7X, (j + 1) * LANES_V7X)
        acc = jnp.broadcast_to(cb_ref[:, cs], (CONV_ROWS, LANES_V7X))
        for k in range(k_taps):
            lo = c0 + first + k
            acc = acc + cw_ref[k:k + 1, cs] * abuf_ref[j, lo:lo + CONV_ROWS, :]
        cbuf_ref[c0:c0 + CONV_ROWS, cs] = acc

    def m_ln(r, c1):
        rs = slice(r * sub + c1, r * sub + c1 + WIDE_ROWS)
        a = _layer_norm(cbuf_ref[rs, :], lag_ref[...], lab_ref[...])
        mix_ref[rs, 0:cw] = (a * _sigmoid(a)).astype(BF16)

    def m_sp(r, blk, j):
        zrows = slice(blk * GMLP_BLOCK, (blk + 1) * GMLP_BLOCK)
        rows = slice(r * sub + blk * GMLP_BLOCK, r * sub + (blk + 1) * GMLP_BLOCK)
        lane = lax.broadcasted_iota(jnp.int32, (GMLP_BLOCK, LANES_V7X), 1)
        even_head = lane < GMLP_HEAD_DIM
        cs = slice(j * LANES_V7X, (j + 1) * LANES_V7X)
        vb = vbuf_ref[rows, cs]
        zero = jnp.zeros_like(vb)
        rhs = jnp.concatenate(
            [jnp.where(even_head, vb, zero), jnp.where(even_head, zero, vb)], axis=0)
        sp = _dot(wsp_ref[j], rhs) + bsp_ref[:, cs]
        u = _gelu(z_ref[r % 2, zrows, 2 * cw + j * LANES_V7X:2 * cw + (j + 1) * LANES_V7X])
        mix_ref[rows, cw + j * LANES_V7X:cw + (j + 1) * LANES_V7X] = (u * sp).astype(BF16)

    def m_outdot(r, n0):
        rows = slice(r * sub, (r + 1) * sub)
        ns = slice(n0, n0 + MXU_COLS_V7X)
        h_ref[0, rows, ns] = x_ref[0, rows, ns] + _dot(mix_ref[rows, :], wout_ref[:, ns])

    def rms_tasks(r):
        return [part(m_rms, r, c0) for c0 in range(0, sub, WIDE_ROWS)]

    def indot_tasks(r):
        return [part(m_indot, r, n0) for n0 in range(0, 4 * cw, MXU_COLS_V7X)]

    def outdot_tasks(r):
        return [part(m_outdot, r, n0) for n0 in range(0, d, MXU_COLS_V7X)]

    def vector_tasks(r):
        tasks = [part(m_glu, r, j) for j in range(ngrp)]
        for c in range(sub // CONV_ROWS):
            tasks += [part(m_conv, r, c, j) for j in range(ngrp)]
            tasks += [part(m_ln, r, c * CONV_ROWS + c1) for c1 in range(0, CONV_ROWS, WIDE_ROWS)]
        tasks += [part(m_v, r, c0) for c0 in range(0, sub, WIDE_ROWS)]
        tasks += [part(m_sp, r, blk, j) for blk in range(sub // GMLP_BLOCK) for j in range(ngrp)]
        return tasks

    first_in = indot_tasks(0)
    for task in rms_tasks(0) + first_in[:len(first_in) // 2]:
        task()
    for r in range(n_sub):
        matmuls = first_in[len(first_in) // 2:] if r == 0 else []
        if r + 1 < n_sub:
            for task in rms_tasks(r + 1):
                task()
            matmuls += indot_tasks(r + 1)
        if r > 0:
            matmuls += outdot_tasks(r - 1)
        _interleave(vector_tasks(r), matmuls)
    for task in outdot_tasks(n_sub - 1):
        task()
    abuf_ref[:, 0:CONV_HIST, :] = abuf_ref[:, ts:ts + CONV_HIST, :]


def _ffn_kernel(h_ref, ng_ref, wu_ref, cw_ref, cb_ref, wd_ref, fg_ref, o_ref,
                ubuf_ref, carry_ref, hid_ref, *, ts, fc, dff):
    t = pl.program_id(1)
    n_chunks = dff // fc
    fgrp = fc // LANES_V7X
    pad = SUBLANES_V7X

    @pl.when(t == 0)
    def _():
        carry_ref[...] = jnp.zeros(carry_ref.shape, F32)

    h = h_ref[0]
    y = _rms(h, ng_ref[...]).astype(BF16)

    def conv3(u, c0, slot):
        outs = []
        for i in range(fgrp):
            gi = c0 // LANES_V7X + i
            ls = slice(c0 + i * LANES_V7X, c0 + (i + 1) * LANES_V7X)
            ui = u[:, i * LANES_V7X:(i + 1) * LANES_V7X]
            ubuf_ref[slot, i, 0:pad, :] = carry_ref[gi]
            ubuf_ref[slot, i, pad:pad + ts, :] = ui
            outs.append(cw_ref[0:1, ls] * ubuf_ref[slot, i, pad - 2:pad - 2 + ts, :]
                        + cw_ref[1:2, ls] * ubuf_ref[slot, i, pad - 1:pad - 1 + ts, :]
                        + cw_ref[2:3, ls] * ui + cb_ref[:, ls])
            carry_ref[gi] = ui[ts - pad:ts]
        return jnp.concatenate(outs, axis=1)

    for c in range(n_chunks):
        gs, vs = c * fc, dff + c * fc
        g = conv3(_dot(y, wu_ref[:, gs:gs + fc]), gs, (2 * c) % UBUF_SLOTS)
        v = conv3(_dot(y, wu_ref[:, vs:vs + fc]), vs, (2 * c + 1) % UBUF_SLOTS)
        hid_ref[:, gs:gs + fc] = (g * _sigmoid(g) * v).astype(BF16)
    hid = hid_ref[...]
    h2 = jnp.concatenate(
        [h[:, n0:n0 + MXU_COLS_V7X] + _dot(hid, wd_ref[:, n0:n0 + MXU_COLS_V7X])
         for n0 in range(0, h.shape[1], MXU_COLS_V7X)], axis=1)
    o_ref[0] = _rms(h2, fg_ref[...])


def _bf16_weight(w):
    w = w.astype(BF16)
    if (w.shape[1] // LANES_V7X) % SUBLANES_V7X == 0:
        w = jnp.pad(w, ((0, 0), (0, LANES_V7X)))
    return w


def _const_spec(shape):
    nd = len(shape)
    return pl.BlockSpec(shape, lambda b, t: (0,) * nd, pipeline_mode=pl.Buffered(1))


def kernel(x, mix_norm_g, w_in, b_in, conv_a_w, conv_a_b, ln_a_g, ln_a_b, ln_b_g, ln_b_b, w_spatial, b_spatial, w_out, ffn_norm_g, w_up, conv_f_w, conv_f_b, w_down, final_norm_g):
    bsz, seq, d = x.shape
    assert w_in.shape[0] == 1
    in_w = w_in.shape[2]
    cw = in_w // 4
    k_taps = conv_a_w.shape[1]
    heads = w_spatial.shape[1]
    dff = w_down.shape[1]
    assert cw == heads * GMLP_HEAD_DIM and w_spatial.shape[2] == GMLP_BLOCK
    assert k_taps - 1 <= CONV_HIST
    assert seq % TS_MIX == 0 and seq % TS_FFN == 0 and TS_MIX % SUB_MIX == 0
    assert SUB_MIX % CONV_ROWS == 0 and CONV_ROWS % WIDE_ROWS == 0 and SUB_MIX % NARROW_ROWS == 0
    assert dff % FFN_CHUNK == 0 and conv_f_w.shape[1] == FFN_TAPS

    row = lambda v: v.reshape(1, -1).astype(F32)

    w_in_b = _bf16_weight(w_in[0])
    w_out_b = _bf16_weight(w_out[0])
    ws_cat = (w_spatial[0].reshape(heads // 2, 2, GMLP_BLOCK, GMLP_BLOCK)
              .transpose(0, 2, 1, 3).reshape(heads // 2, GMLP_BLOCK, 2 * GMLP_BLOCK))
    bsp = jnp.repeat(jnp.transpose(b_spatial[0]), GMLP_HEAD_DIM, axis=1)

    def cparams(n_inputs, weight_operands):
        return pltpu.CompilerParams(
            dimension_semantics=("arbitrary", "arbitrary"),
            vmem_limit_bytes=VMEM_LIMIT_BYTES_V7X,
            allow_input_fusion=[i in weight_operands for i in range(n_inputs)])

    ts = TS_MIX
    h1 = pl.pallas_call(
        functools.partial(_mixer_kernel, ts=ts, cw=cw, k_taps=k_taps),
        out_shape=jax.ShapeDtypeStruct((bsz, seq, d), F32),
        grid=(bsz, seq // ts),
        in_specs=[
            pl.BlockSpec((1, ts, d), lambda b, t: (b, t, 0)),
            _const_spec((1, d)),
            _const_spec(w_in_b.shape),
            _const_spec((1, in_w)),
            _const_spec((k_taps, cw)),
            _const_spec((1, cw)),
            _const_spec((1, cw)),
            _const_spec((1, cw)),
            _const_spec((1, cw)),
            _const_spec((1, cw)),
            _const_spec((heads // 2, GMLP_BLOCK, 2 * GMLP_BLOCK)),
            _const_spec((GMLP_BLOCK, cw)),
            _const_spec(w_out_b.shape),
        ],
        out_specs=pl.BlockSpec((1, ts, d), lambda b, t: (b, t, 0)),
        scratch_shapes=[
            pltpu.VMEM((heads // 2, GMLP_BLOCK, 2 * GMLP_BLOCK), BF16),
            pltpu.VMEM((2, SUB_MIX, d), BF16),
            pltpu.VMEM((2, SUB_MIX, in_w), F32),
            pltpu.VMEM((cw // LANES_V7X, CONV_HIST + ts, LANES_V7X), F32),
            pltpu.VMEM((ts, cw), F32),
            pltpu.VMEM((ts, cw), BF16),
            pltpu.VMEM((ts, 2 * cw), BF16),
        ],
        compiler_params=cparams(13, (2, 12)),
        name="mixer",
    )(x, row(mix_norm_g[0]), w_in_b, row(b_in[0]), conv_a_w[0], row(conv_a_b[0]),
      row(ln_a_g[0]), row(ln_a_b[0]), row(ln_b_g[0]), row(ln_b_b[0]), ws_cat, bsp, w_out_b)

    ts = TS_FFN
    fc = FFN_CHUNK
    w_up_b = _bf16_weight(w_up[0])
    w_down_b = w_down[0].astype(BF16)
    out = pl.pallas_call(
        functools.partial(_ffn_kernel, ts=ts, fc=fc, dff=dff),
        out_shape=jax.ShapeDtypeStruct((bsz, seq, d), F32),
        grid=(bsz, seq // ts),
        in_specs=[
            pl.BlockSpec((1, ts, d), lambda b, t: (b, t, 0)),
            _const_spec((1, d)),
            _const_spec(w_up_b.shape),
            _const_spec((FFN_TAPS, 2 * dff)),
            _const_spec((1, 2 * dff)),
            _const_spec(w_down_b.shape),
            _const_spec((1, d)),
        ],
        out_specs=pl.BlockSpec((1, ts, d), lambda b, t: (b, t, 0)),
        scratch_shapes=[
            pltpu.VMEM((UBUF_SLOTS, fc // LANES_V7X, SUBLANES_V7X + ts, LANES_V7X), F32),
            pltpu.VMEM((2 * dff // LANES_V7X, SUBLANES_V7X, LANES_V7X), F32),
            pltpu.VMEM((ts, dff), BF16),
        ],
        compiler_params=cparams(7, (2, 5)),
        name="ffn",
    )(h1, row(ffn_norm_g[0]), w_up_b, conv_f_w[0], row(conv_f_b[0]), w_down_b,
      row(final_norm_g))
    return out
```
